```python
import jax, jax.numpy as jnp
from jax import lax
import numpy as np

D_MODEL = 1024
BATCH = 8
SEQ = 4096
DEPTH = 4

CTX_LEN = 256
GRID_W = 64
N_MIXERS = 2
EXPAND = 2
D_INNER = EXPAND * D_MODEL
MLSTM_HEADS = 4
MLSTM_HEAD_DIM = D_INNER // MLSTM_HEADS
CHUNK = 64
CONV_K = 3
FOURIER_GROUPS = 8
FOURIER_GROUP_DIM = D_INNER // FOURIER_GROUPS
N_MLSTM_LAYERS = (DEPTH + 1) // 2
N_FOURIER_LAYERS = DEPTH // 2
LAST_MLSTM = ((DEPTH - 1) // N_MIXERS) * N_MIXERS
EPS = 1e-6

kernel_name = "mlstm_fourier_hybrid_dit_prefix"

H = MLSTM_HEADS
DH = MLSTM_HEAD_DIM


def rmsnorm(x, g):
    xf = x.astype(jnp.float32)
    y = xf * lax.rsqrt(jnp.mean(xf * xf, axis=-1, keepdims=True) + EPS)
    return (y * g.astype(jnp.float32)).astype(x.dtype)


def mlstm_chunk_scan(q, k, v, li, lf, state, emit):
    b, h, n, d = q.shape
    nc = n // CHUNK

    def to_chunks(a):
        a = a.reshape(b, h, nc, CHUNK, *a.shape[3:])
        return jnp.moveaxis(a, 2, 0)

    lower = jnp.tril(jnp.ones((CHUNK, CHUNK), dtype=bool))

    def step(carry, inp):
        C, nv, m = carry
        qc, kc, vc, lic, lfc = inp
        bcum = jnp.cumsum(lfc, axis=-1)
        logw = bcum[..., :, None] - bcum[..., None, :] + lic[..., None, :]
        logw = jnp.where(lower, logw, -jnp.inf)
        log_prev = bcum + m[..., None]
        m_new = jnp.maximum(log_prev, jnp.max(logw, axis=-1))
        w_intra = jnp.exp(logw - m_new[..., None])
        w_prev = jnp.exp(log_prev - m_new)
        out = None
        if emit:
            scores = jnp.einsum('bhjd,bhsd->bhjs', qc, kc) * w_intra
            num = jnp.einsum('bhjs,bhsd->bhjd', scores, vc) + \
                w_prev[..., None] * jnp.einsum('bhvk,bhjk->bhjv', C, qc)
            den = jnp.sum(scores, axis=-1) + w_prev * jnp.einsum('bhk,bhjk->bhj', nv, qc)
            out = num / jnp.maximum(jnp.abs(den), jnp.exp(-m_new))[..., None]
        m_end = m_new[..., -1]
        w_src = jnp.exp(bcum[..., -1:] - bcum + lic - m_end[..., None])
        decay = jnp.exp(bcum[..., -1] + m - m_end)
        C_new = decay[..., None, None] * C + jnp.einsum('bhsv,bhsk->bhvk', w_src[..., None] * vc, kc)
        n_new = decay[..., None] * nv + jnp.einsum('bhs,bhsk->bhk', w_src, kc)
        return (C_new, n_new, m_end), out

    state, hs = lax.scan(step, state, tuple(to_chunks(a) for a in (q, k, v, li, lf)))
    if emit:
        hs = jnp.moveaxis(hs, 0, 2).reshape(b, h, n, d)
    return hs, state


def mlstm_inputs(hmod, grid_rows, grid_cols, w_in, conv_w, conv_b, w_q, w_k, w_if, b_i, b_f):
    b, n, _ = hmod.shape
    xm, o_pre, z = jnp.split(hmod @ w_in, 3, axis=-1)
    xg = xm.reshape(b, grid_rows, grid_cols, D_INNER)
    xconv = lax.conv_general_dilated(xg, conv_w, (1, 1), 'SAME',
                                     dimension_numbers=('NHWC', 'HWIO', 'NHWC'),
                                     feature_group_count=D_INNER)
    xconv = jax.nn.silu(xconv.reshape(b, n, D_INNER) + conv_b)
    xh = xconv.reshape(b, n, H, DH)
    q = jnp.einsum('bnhd,hde->bhne', xh, w_q).astype(jnp.float32)
    k = (jnp.einsum('bnhd,hde->bhne', xh, w_k) * (DH ** -0.5)).astype(jnp.float32)
    v = xm.reshape(b, n, H, DH).transpose(0, 2, 1, 3).astype(jnp.float32)
    wq_g = w_if[:, 0].reshape(2, H, DH, 2 * H).astype(jnp.float32)
    wk_g = w_if[:, 1].reshape(2, H, DH, 2 * H).astype(jnp.float32)
    wv_g = w_if[:, 2].reshape(2, H, DH, 2 * H).astype(jnp.float32)
    gates = (jnp.einsum('bhne,rheg->rbng', q, wq_g) + jnp.einsum('bhne,rheg->rbng', k, wk_g)
             + jnp.einsum('bhne,rheg->rbng', v, wv_g))
    li = gates[..., :H] + b_i.astype(jnp.float32)[:, None, None, :]
    lf = jax.nn.log_sigmoid(gates[..., H:] + b_f.astype(jnp.float32)[:, None, None, :])
    li = jnp.moveaxis(li, -1, 2)
    lf = jnp.moveaxis(lf, -1, 2)
    return (q, k, v, li, lf), o_pre, z, xconv


def mlstm_output(hsum, o_pre, z, xconv, ln_w, skip, w_out, dtype):
    b, _, n, _ = hsum.shape
    hh = hsum.transpose(0, 2, 1, 3)
    mu = jnp.mean(hh, axis=-1, keepdims=True)
    var = jnp.mean(jnp.square(hh - mu), axis=-1, keepdims=True)
    hn = ((hh - mu) * lax.rsqrt(var + EPS)).reshape(b, n, D_INNER) * ln_w.astype(jnp.float32)
    hn = jax.nn.sigmoid(o_pre.astype(jnp.float32)) * hn
    y = (hn.astype(dtype) + skip * xconv) * jax.nn.silu(z)
    return y @ w_out


def mlstm_branch(h_lat, h_ctx, rows, emit_ctx, w_in, conv_w, conv_b, w_q, w_k, w_if,
                 b_i, b_f, ln_w, skip, w_out):
    lat, o_l, z_l, xc_l = mlstm_inputs(h_lat, rows, GRID_W, w_in, conv_w, conv_b, w_q, w_k, w_if, b_i, b_f)
    ctx, o_c, z_c, xc_c = mlstm_inputs(h_ctx, 1, h_ctx.shape[1], w_in, conv_w, conv_b, w_q, w_k, w_if, b_i, b_f)
    b = h_lat.shape[0]
    zero = (jnp.zeros((b, H, DH, DH), jnp.float32), jnp.zeros((b, H, DH), jnp.float32),
            jnp.zeros((b, H), jnp.float32))
    h_lat_sum = 0.0
    h_ctx_sum = 0.0
    for r, rev in ((0, False), (1, True)):
        fl = (lambda a: jnp.flip(a, 2)) if rev else (lambda a: a)
        qc, kc, vc, lic, lfc = ctx
        ql, kl, vl, lil, lfl = lat
        hc, st = mlstm_chunk_scan(fl(qc), fl(kc), fl(vc), fl(lic[r]), fl(lfc[r]), zero, emit_ctx)
        hl, _ = mlstm_chunk_scan(fl(ql), fl(kl), fl(vl), fl(lil[r]), fl(lfl[r]), st, True)
        h_lat_sum = h_lat_sum + fl(hl)
        if emit_ctx:
            h_ctx_sum = h_ctx_sum + fl(hc)
    y_lat = mlstm_output(h_lat_sum, o_l, z_l, xc_l, ln_w, skip, w_out, h_lat.dtype)
    y_ctx = mlstm_output(h_ctx_sum, o_c, z_c, xc_c, ln_w, skip, w_out, h_ctx.dtype) if emit_ctx else None
    return y_lat, y_ctx


def fourier_branch(hmod, w_in, w_out):
    b, n, _ = hmod.shape
    u, z = jnp.split(hmod @ w_in, 2, axis=-1)
    ug = u.astype(jnp.float32).reshape(b, n, FOURIER_GROUPS, FOURIER_GROUP_DIM)
    f = jnp.fft.fft2(ug, axes=(1, 3), norm='ortho').real
    y = f.reshape(b, n, D_INNER).astype(hmod.dtype) * jax.nn.silu(z)
    return y @ w_out


def setup_inputs(seed: int = 0) -> dict:
    key = jax.random.key(seed)
    ks = jax.random.split(key, 24)
    nA, nB, E = N_MLSTM_LAYERS, N_FOURIER_LAYERS, D_INNER
    nrm = jax.random.normal
    return {
        "x": nrm(ks[0], (BATCH, SEQ, D_MODEL), jnp.float32),
        "c": nrm(ks[1], (BATCH, D_MODEL), jnp.float32),
        "ctx": nrm(ks[2], (BATCH, CTX_LEN, D_MODEL), jnp.float32),
        "c_ctx": nrm(ks[3], (D_MODEL,), jnp.float32),
        "norm_g": 1.0 + 0.1 * nrm(ks[4], (DEPTH, D_MODEL), jnp.float32),
        "w_ada": nrm(ks[5], (DEPTH, D_MODEL, 3 * D_MODEL), jnp.float32) * D_MODEL ** -0.5,
        "b_ada": 0.02 * nrm(ks[6], (DEPTH, 3 * D_MODEL), jnp.float32),
        "m_w_in": nrm(ks[7], (nA, D_MODEL, 3 * E), jnp.float32) * D_MODEL ** -0.5,
        "m_conv_w": nrm(ks[8], (nA, CONV_K, CONV_K, 1, E), jnp.float32) / CONV_K,
        "m_conv_b": 0.02 * nrm(ks[9], (nA, E), jnp.float32),
        "m_w_q": nrm(ks[10], (nA, H, DH, DH), jnp.float32) * DH ** -0.5,
        "m_w_k": nrm(ks[11], (nA, H, DH, DH), jnp.float32) * DH ** -0.5,
        "m_w_if": nrm(ks[12], (nA, 2, 3, E, 2 * H), jnp.float32) * (3 * E) ** -0.5,
        "m_b_i": 0.1 * nrm(ks[13], (nA, 2, H), jnp.float32),
        "m_b_f": jnp.linspace(3.0, 6.0, H, dtype=jnp.float32) + 0.1 * nrm(ks[14], (nA, 2, H), jnp.float32),
        "m_ln_w": 1.0 + 0.1 * nrm(ks[15], (nA, E), jnp.float32),
        "m_skip": 1.0 + 0.1 * nrm(ks[16], (nA, E), jnp.float32),
        "m_w_out": nrm(ks[17], (nA, E, D_MODEL), jnp.float32) * E ** -0.5,
        "f_w_in": nrm(ks[18], (nB, D_MODEL, 2 * E), jnp.float32) * D_MODEL ** -0.5,
        "f_w_out": nrm(ks[19], (nB, E, D_MODEL), jnp.float32) * E ** -0.5,
        "norm_f": 1.0 + 0.1 * nrm(ks[20], (D_MODEL,), jnp.float32),
    }


def reference(x, c, ctx, c_ctx, norm_g, w_ada, b_ada, m_w_in, m_conv_w, m_conv_b, m_w_q, m_w_k,
              m_w_if, m_b_i, m_b_f, m_ln_w, m_skip, m_w_out, f_w_in, f_w_out, norm_f):
    rows = x.shape[1] // GRID_W
    xc = ctx
    sc_lat = jax.nn.silu(c)
    sc_ctx = jax.nn.silu(c_ctx)
    for i in range(DEPTH):
        j = i // N_MIXERS
        update_ctx = i < LAST_MLSTM
        sh, sc, g = jnp.split(sc_lat @ w_ada[i] + b_ada[i], 3, axis=-1)
        h = rmsnorm(x, norm_g[i]) * (1.0 + sc[:, None, :]) + sh[:, None, :]
        need_ctx_in = update_ctx or (i % N_MIXERS == 0)
        if need_ctx_in:
            shc, scc, gc = jnp.split(sc_ctx @ w_ada[i] + b_ada[i], 3, axis=-1)
            hc = rmsnorm(xc, norm_g[i]) * (1.0 + scc) + shc
        if i % N_MIXERS == 0:
            y, yc = mlstm_branch(h, hc, rows, update_ctx, m_w_in[j], m_conv_w[j], m_conv_b[j],
                                 m_w_q[j], m_w_k[j], m_w_if[j], m_b_i[j], m_b_f[j],
                                 m_ln_w[j], m_skip[j], m_w_out[j])
        else:
            y = fourier_branch(h, f_w_in[j], f_w_out[j])
            yc = fourier_branch(hc, f_w_in[j], f_w_out[j]) if update_ctx else None
        x = x + g[:, None, :] * y
        if update_ctx:
            xc = xc + gc * yc
    return rmsnorm(x, norm_f)
```

```python
import functools

import numpy as np
import jax
import jax.numpy as jnp
from jax import lax
from jax.experimental import pallas as pl
from jax.experimental.pallas import tpu as pltpu

F32 = jnp.float32
BF16 = jnp.bfloat16

GRID_W = 64
N_MIXERS = 2
MLSTM_HEADS = 4
FOURIER_GROUPS = 8
EPS = 1e-6
SCAN_CHUNK = 256
GATE_LANES = 128
VMEM_LIMIT = 56 * 1024 * 1024


def _cparams(*sem):
    return pltpu.CompilerParams(dimension_semantics=sem, vmem_limit_bytes=VMEM_LIMIT)


def _sigmoid(v):
    return 1.0 / (1.0 + jnp.exp(-v))


def _silu(v):
    return v * _sigmoid(v)


def _dot(a, b):
    return jnp.dot(a, b, preferred_element_type=F32)


def _dot_nt(a, b):
    return lax.dot_general(a, b, (((1,), (1,)), ((), ())), preferred_element_type=F32)


def _ada_kernel(c_ref, w_ref, b_ref, o_ref):
    s = _silu(c_ref[...])
    o_ref[0] = _dot(s.astype(BF16), w_ref[0].astype(BF16)) + b_ref[0]


def _ada(cc, w_ada, b_ada):
    depth, d, d3 = w_ada.shape
    r = cc.shape[0]
    tn = 512
    return pl.pallas_call(
        _ada_kernel,
        grid=(depth, d3 // tn),
        in_specs=[pl.BlockSpec((r, d), lambda i, j: (0, 0)),
                  pl.BlockSpec((1, d, tn), lambda i, j: (i, 0, j)),
                  pl.BlockSpec((1, 1, tn), lambda i, j: (i, 0, j))],
        out_specs=pl.BlockSpec((1, r, tn), lambda i, j: (i, 0, j)),
        out_shape=jax.ShapeDtypeStruct((depth, r, d3), F32),
        compiler_params=_cparams("parallel", "parallel"),
        name="ada",
    )(cc, w_ada, b_ada.reshape(depth, 1, d3))


def _norm_matmul_kernel(x_ref, mod_ref, g_ref, w_ref, o_ref, *, d):
    x = x_ref[0]
    y = x * lax.rsqrt(jnp.mean(x * x, axis=-1, keepdims=True) + EPS) * g_ref[...]
    mod = mod_ref[0]
    h = y * (1.0 + mod[:, d:2 * d]) + mod[:, :d]
    o_ref[0] = _dot(h.astype(BF16), w_ref[...]).astype(o_ref.dtype)


def _norm_matmul(x, mod, gain, w, *, tm, tn):
    b, n, d = x.shape
    no = w.shape[1]
    per_batch = mod.shape[0] != 1
    mod_map = (lambda j, bi, t: (bi, 0, 0)) if per_batch else (lambda j, bi, t: (0, 0, 0))
    return pl.pallas_call(
        functools.partial(_norm_matmul_kernel, d=d),
        grid=(no // tn, b, n // tm),
        in_specs=[pl.BlockSpec((1, tm, d), lambda j, bi, t: (bi, t, 0)),
                  pl.BlockSpec((1, 1, 3 * d), mod_map),
                  pl.BlockSpec((1, d), lambda j, bi, t: (0, 0)),
                  pl.BlockSpec((d, tn), lambda j, bi, t: (0, j))],
        out_specs=pl.BlockSpec((1, tm, tn), lambda j, bi, t: (bi, t, j)),
        out_shape=jax.ShapeDtypeStruct((b, n, no), BF16),
        compiler_params=_cparams("parallel", "parallel", "parallel"),
        name="norm_matmul",
    )(x, mod, gain, w)


def _conv_qk_kernel(xc_ref, xp_ref, xn_ref, cw_ref, cb_ref, wq_ref, wk_ref, wkt_ref, wg_ref,
                    xconv_ref, q_ref, k_ref, kt_ref, gates_ref, *, tm, img_w, has_vert, n_tiles, scale):
    t = pl.program_id(1)
    h = pl.program_id(2)
    xcb = xc_ref[0]
    xc = xcb.astype(F32)
    if has_vert:
        xp = jnp.where(t > 0, xp_ref[0].astype(F32), 0.0)
        xn = jnp.where(t < n_tiles - 1, xn_ref[0].astype(F32), 0.0)
        ext = jnp.concatenate([xp, xc, xn], axis=0)
        off = img_w
        taps = (-1, 0, 1)
    else:
        ext = xc
        off = 0
        taps = (0,)
    rows = ext.shape[0]
    col = lax.broadcasted_iota(jnp.int32, (rows, 1), 0) % img_w
    left = jnp.where(col == 0, 0.0, pltpu.roll(ext, 1, 0))
    right = jnp.where(col == img_w - 1, 0.0, pltpu.roll(ext, rows - 1, 0))
    acc = None
    for dr in taps:
        s = off + dr * img_w
        r = 3 * (dr + 1)
        term = (cw_ref[r:r + 1, :] * left[s:s + tm] + cw_ref[r + 1:r + 2, :] * ext[s:s + tm]
                + cw_ref[r + 2:r + 3, :] * right[s:s + tm])
        acc = term if acc is None else acc + term
    xv = _silu(acc + cb_ref[...]).astype(BF16)
    xconv_ref[0] = xv
    qb = _dot(xv, wq_ref[0]).astype(BF16)
    kb = (_dot(xv, wk_ref[0]) * scale).astype(BF16)
    q_ref[0] = qb
    k_ref[0] = kb
    kt_ref[0] = (_dot_nt(wkt_ref[0], xv) * scale).astype(BF16)
    g = _dot(qb, wg_ref[0, 0]) + _dot(kb, wg_ref[0, 1]) + _dot(xcb, wg_ref[0, 2])

    @pl.when(h == 0)
    def _():
        gates_ref[0] = g

    @pl.when(h > 0)
    def _():
        gates_ref[0] += g


def _conv_qk(a, cw, cb, wq, wk, wkt, wg, *, tm, img_w, has_vert):
    b, n, _ = a.shape
    heads, dh, _ = wq.shape
    e = heads * dh
    n_tiles = n // tm
    hb = img_w if has_vert else 16
    per = tm // hb
    last = n // hb - 1
    kern = functools.partial(_conv_qk_kernel, tm=tm, img_w=img_w, has_vert=has_vert, n_tiles=n_tiles,
                             scale=float(dh) ** -0.5)
    tok = lambda bi, t, h: (bi, t, h)
    return pl.pallas_call(
        kern,
        grid=(b, n_tiles, heads),
        in_specs=[pl.BlockSpec((1, tm, dh), tok),
                  pl.BlockSpec((1, hb, dh), lambda bi, t, h: (bi, jnp.maximum(t * per - 1, 0), h)),
                  pl.BlockSpec((1, hb, dh), lambda bi, t, h: (bi, jnp.minimum((t + 1) * per, last), h)),
                  pl.BlockSpec((9, dh), lambda bi, t, h: (0, h)),
                  pl.BlockSpec((1, dh), lambda bi, t, h: (0, h)),
                  pl.BlockSpec((1, dh, dh), lambda bi, t, h: (h, 0, 0)),
                  pl.BlockSpec((1, dh, dh), lambda bi, t, h: (h, 0, 0)),
                  pl.BlockSpec((1, dh, dh), lambda bi, t, h: (h, 0, 0)),
                  pl.BlockSpec((1, 3, dh, GATE_LANES), lambda bi, t, h: (h, 0, 0, 0))],
        out_specs=[pl.BlockSpec((1, tm, dh), tok),
                   pl.BlockSpec((1, tm, dh), tok),
                   pl.BlockSpec((1, tm, dh), tok),
                   pl.BlockSpec((1, dh, tm), lambda bi, t, h: (bi, h, t)),
                   pl.BlockSpec((1, tm, GATE_LANES), lambda bi, t, h: (bi, t, 0))],
        out_shape=[jax.ShapeDtypeStruct((b, n, e), BF16),
                   jax.ShapeDtypeStruct((b, n, e), BF16),
                   jax.ShapeDtypeStruct((b, n, e), BF16),
                   jax.ShapeDtypeStruct((b, e, n), BF16),
                   jax.ShapeDtypeStruct((b, n, GATE_LANES), F32)],
        compiler_params=_cparams("parallel", "parallel", "arbitrary"),
        name="conv_qk",
    )(a, a, a, cw, cb, wq, wk, wkt, wg)


def _split3(a):
    hi = a.astype(BF16)
    r1 = a - hi.astype(F32)
    mid = r1.astype(BF16)
    lo = (r1 - mid.astype(F32)).astype(BF16)
    return hi, mid, lo


def _gate_fin_kernel(g_ref, bias_ref, gcol_ref, bcol_ref, brow_ref, *, L, heads):
    nd = 2 * heads
    pre = g_ref[0] + bias_ref[...]
    lf = jnp.minimum(pre, 0.0) - jnp.log(1.0 + jnp.exp(-jnp.abs(pre)))
    row = lax.broadcasted_iota(jnp.int32, (L, L), 0)
    col = lax.broadcasted_iota(jnp.int32, (L, L), 1)
    tri_l = (col <= row).astype(BF16)
    tri_u = (col >= row).astype(BF16)
    parts = _split3(lf)
    g_l = _dot(tri_l, parts[0]) + _dot(tri_l, parts[1]) + _dot(tri_l, parts[2])
    g_u = _dot(tri_u, parts[0]) + _dot(tri_u, parts[1]) + _dot(tri_u, parts[2])
    lane = lax.broadcasted_iota(jnp.int32, (L, GATE_LANES), 1)
    g_sel = jnp.where(lane < nd + heads, g_l, g_u)
    g = pltpu.roll(g_sel, GATE_LANES - nd, 1)
    bc = pre - g
    gcol_ref[0] = g
    bcol_ref[0] = bc
    brow_ref[0] = jnp.transpose(bc)[0:8, :]


def _gate_fin(gates, bias, *, heads):
    b, n, _ = gates.shape
    L = SCAN_CHUNK
    assert 2 * heads == 8
    blk = pl.BlockSpec((1, L, GATE_LANES), lambda bi, c: (bi, c, 0))
    return pl.pallas_call(
        functools.partial(_gate_fin_kernel, L=L, heads=heads),
        grid=(b, n // L),
        in_specs=[blk, pl.BlockSpec((1, GATE_LANES), lambda bi, c: (0, 0))],
        out_specs=[blk, blk, pl.BlockSpec((1, 8, L), lambda bi, c: (bi, 0, c))],
        out_shape=[jax.ShapeDtypeStruct((b, n, GATE_LANES), F32),
                   jax.ShapeDtypeStruct((b, n, GATE_LANES), F32),
                   jax.ShapeDtypeStruct((b, 8, n), F32)],
        compiler_params=_cparams("parallel", "parallel"),
        name="gate_fin",
    )(gates, bias)


def _chain_step(q_ref, k_ref, kt_ref, v_ref, gc_ref, bc_ref, br_ref, out_ref, ct_ref, n_ref, m_ref,
                *, slot, lane_idx, lower, L):
    q = q_ref[0]
    kt = kt_ref[0]
    v = v_ref[0]
    lane = lax.broadcasted_iota(jnp.int32, (L, GATE_LANES), 1)
    gcol = jnp.sum(jnp.where(lane == lane_idx, gc_ref[0], 0.0), axis=1, keepdims=True)
    bcol = jnp.sum(jnp.where(lane == lane_idx, bc_ref[0], 0.0), axis=1, keepdims=True)
    brow = br_ref[0, pl.ds(lane_idx, 1), :]
    m_prev = m_ref[slot][:, 0:1]

    row = lax.broadcasted_iota(jnp.int32, (L, L), 0)
    col = lax.broadcasted_iota(jnp.int32, (L, L), 1)
    mask = (col <= row) if lower else (col >= row)
    logw = jnp.where(mask, gcol + brow, -jnp.inf)
    log_prev = gcol + m_prev
    m_new = jnp.maximum(log_prev, jnp.max(logw, axis=1, keepdims=True))
    w_intra = jnp.exp(logw - m_new)
    w_prev = jnp.exp(log_prev - m_new)

    if out_ref is not None:
        p = _dot(q, kt) * w_intra
        n_row = n_ref[slot]
        qn = jnp.sum(q.astype(F32) * n_row, axis=1, keepdims=True)
        den = jnp.sum(p, axis=1, keepdims=True) + w_prev * qn
        num = _dot(p.astype(BF16), v) + w_prev * _dot(q, ct_ref[slot].astype(BF16))
        inv = 1.0 / jnp.maximum(jnp.abs(den), jnp.exp(-m_new))
        out_ref[0] = (num * inv).astype(out_ref.dtype)

    rowv = lax.broadcasted_iota(jnp.int32, (L, 1), 0)
    last = (rowv == (L - 1 if lower else 0))
    g_last = jnp.sum(jnp.where(last, gcol, 0.0), axis=0, keepdims=True)
    m_end = jnp.sum(jnp.where(last, m_new, 0.0), axis=0, keepdims=True)
    w_src = jnp.exp(bcol + g_last - m_end)
    decay = jnp.exp(g_last + m_prev - m_end)
    vw = (w_src * v.astype(F32)).astype(BF16)
    ct_ref[slot] = decay * ct_ref[slot] + _dot(kt, vw)
    n_ref[slot] = decay * n_ref[slot] + jnp.sum(w_src * k_ref[0].astype(F32), axis=0, keepdims=True)
    m_ref[slot] = jnp.broadcast_to(m_end, m_ref.shape[1:])


def _scan_kernel(*refs, L, heads, emit_ctx):
    ctx_in = refs[0:7]
    fwd_in = refs[7:14]
    rev_in = refs[14:21]
    hf_ref, hr_ref = refs[21:23]
    if emit_ctx:
        hcf_ref, hcr_ref = refs[23:25]
        ct_ref, n_ref, m_ref = refs[25:28]
    else:
        hcf_ref = hcr_ref = None
        ct_ref, n_ref, m_ref = refs[23:26]
    h = pl.program_id(1)
    s = pl.program_id(2)
    step = functools.partial(_chain_step, ct_ref=ct_ref, n_ref=n_ref, m_ref=m_ref, L=L)

    @pl.when(s == 0)
    def _():
        ct_ref[...] = jnp.zeros_like(ct_ref)
        n_ref[...] = jnp.zeros_like(n_ref)
        m_ref[...] = jnp.zeros_like(m_ref)
        step(*ctx_in, hcf_ref, slot=0, lane_idx=h, lower=True)
        step(*ctx_in, hcr_ref, slot=1, lane_idx=heads + h, lower=False)

    @pl.when(s > 0)
    def _():
        step(*fwd_in, hf_ref, slot=0, lane_idx=h, lower=True)
        step(*rev_in, hr_ref, slot=1, lane_idx=heads + h, lower=False)


def _scan(lat, ctx, *, heads, emit_ctx):
    b, n, e = lat["q"].shape
    nc = ctx["q"].shape[1]
    L = SCAN_CHUNK
    assert nc == L, "the context prefix is processed as exactly one chunk"
    dh = e // heads
    n_lat = n // L

    def specs(imap):
        tok = lambda bi, h, s: (bi, imap(s), h)
        return [pl.BlockSpec((1, L, dh), tok),
                pl.BlockSpec((1, L, dh), tok),
                pl.BlockSpec((1, dh, L), lambda bi, h, s: (bi, h, imap(s))),
                pl.BlockSpec((1, L, dh), tok),
                pl.BlockSpec((1, L, GATE_LANES), lambda bi, h, s: (bi, imap(s), 0)),
                pl.BlockSpec((1, L, GATE_LANES), lambda bi, h, s: (bi, imap(s), 0)),
                pl.BlockSpec((1, 8, L), lambda bi, h, s: (bi, 0, imap(s)))]

    def args(d):
        return [d["q"], d["k"], d["kt"], d["a"], d["gcol"], d["bcol"], d["brow"]]

    ctx_map = lambda s: 0
    fwd_map = lambda s: jnp.maximum(s - 1, 0)
    rev_map = lambda s: n_lat - jnp.maximum(s, 1)
    out_specs = [pl.BlockSpec((1, L, dh), lambda bi, h, s: (bi, fwd_map(s), h)),
                 pl.BlockSpec((1, L, dh), lambda bi, h, s: (bi, rev_map(s), h))]
    out_shape = [jax.ShapeDtypeStruct((b, n, e), BF16), jax.ShapeDtypeStruct((b, n, e), BF16)]
    if emit_ctx:
        out_specs += [pl.BlockSpec((1, L, dh), lambda bi, h, s: (bi, 0, h))] * 2
        out_shape += [jax.ShapeDtypeStruct((b, nc, e), BF16)] * 2
    return pl.pallas_call(
        functools.partial(_scan_kernel, L=L, heads=heads, emit_ctx=emit_ctx),
        grid=(b, heads, n_lat + 1),
        in_specs=specs(ctx_map) + specs(fwd_map) + specs(rev_map),
        out_specs=out_specs,
        out_shape=out_shape,
        scratch_shapes=[pltpu.VMEM((2, dh, dh), F32),
                        pltpu.VMEM((2, 1, dh), F32),
                        pltpu.VMEM((2, 1, GATE_LANES), F32)],
        compiler_params=_cparams("parallel", "parallel", "arbitrary"),
        name="mlstm_scan",
    )(*(args(ctx) + args(lat) + args(lat)))


def _mlstm_out_kernel(hf_ref, hr_ref, o_ref, z_ref, xc_ref, x_ref, lnw_ref, skip_ref, w_ref, mod_ref, out_ref,
                      *, d, dh, heads):
    hs = hf_ref[0].astype(F32) + hr_ref[0].astype(F32)
    parts = []
    for hh in range(heads):
        seg = hs[:, hh * dh:(hh + 1) * dh]
        cen = seg - jnp.mean(seg, axis=-1, keepdims=True)
        var = jnp.mean(cen * cen, axis=-1, keepdims=True)
        parts.append(cen * lax.rsqrt(var + EPS))
    hn = jnp.concatenate(parts, axis=1) * lnw_ref[...]
    hn = _sigmoid(o_ref[0].astype(F32)) * hn
    y = (hn + skip_ref[...] * xc_ref[0].astype(F32)) * _silu(z_ref[0].astype(F32))
    yo = _dot(y.astype(BF16), w_ref[...])
    out_ref[0] = x_ref[0] + mod_ref[0][:, 2 * d:3 * d] * yo


def _mlstm_out(hf, hr, a, xconv, x, lnw, skip, w, mod, *, tm, heads):
    b, n, d = x.shape
    e = hf.shape[2]
    per_batch = mod.shape[0] != 1
    mod_map = (lambda bi, t: (bi, 0, 0)) if per_batch else (lambda bi, t: (0, 0, 0))
    tok = lambda bi, t: (bi, t, 0)
    vec = pl.BlockSpec((1, e), lambda bi, t: (0, 0))
    return pl.pallas_call(
        functools.partial(_mlstm_out_kernel, d=d, dh=e // heads, heads=heads),
        grid=(b, n // tm),
        in_specs=[pl.BlockSpec((1, tm, e), tok),
                  pl.BlockSpec((1, tm, e), tok),
                  pl.BlockSpec((1, tm, e), lambda bi, t: (bi, t, 1)),
                  pl.BlockSpec((1, tm, e), lambda bi, t: (bi, t, 2)),
                  pl.BlockSpec((1, tm, e), tok),
                  pl.BlockSpec((1, tm, d), tok),
                  vec, vec,
                  pl.BlockSpec((e, d), lambda bi, t: (0, 0)),
                  pl.BlockSpec((1, 1, 3 * d), mod_map)],
        out_specs=pl.BlockSpec((1, tm, d), tok),
        out_shape=jax.ShapeDtypeStruct((b, n, d), F32),
        compiler_params=_cparams("parallel", "parallel"),
        name="mlstm_out",
    )(hf, hr, a, a, xconv, x, lnw, skip, w, mod)


def _chan_dft_kernel(u_ref, w_ref, a_ref, b_ref, *, groups, gd):
    for g in range(groups):
        r = _dot(u_ref[0, :, g * gd:(g + 1) * gd], w_ref[...])
        a_ref[0, :, g * gd:(g + 1) * gd] = r[:, :gd].astype(BF16)
        b_ref[0, :, g * gd:(g + 1) * gd] = r[:, gd:].astype(BF16)


def _chan_dft(uz, wc, *, tm, groups):
    b, n, e2 = uz.shape
    e = e2 // 2
    gd = e // groups
    tok = lambda bi, t: (bi, t, 0)
    return pl.pallas_call(
        functools.partial(_chan_dft_kernel, groups=groups, gd=gd),
        grid=(b, n // tm),
        in_specs=[pl.BlockSpec((1, tm, e), tok), pl.BlockSpec((gd, 2 * gd), lambda bi, t: (0, 0))],
        out_specs=[pl.BlockSpec((1, tm, e), tok)] * 2,
        out_shape=[jax.ShapeDtypeStruct((b, n, e), BF16)] * 2,
        compiler_params=_cparams("parallel", "parallel"),
        name="chan_dft",
    )(uz, wc)


def _pos_dft_kernel(cn_ref, sn_ref, a_ref, b_ref, y_ref):
    y_ref[0] = (_dot(cn_ref[...], a_ref[0]) - _dot(sn_ref[...], b_ref[0])).astype(y_ref.dtype)


def _pos_dft(cn, sn, a, bm, *, tm, tn):
    b, n, e = a.shape
    return pl.pallas_call(
        _pos_dft_kernel,
        grid=(b, n // tm, e // tn),
        in_specs=[pl.BlockSpec((tm, n), lambda bi, i, j: (i, 0)),
                  pl.BlockSpec((tm, n), lambda bi, i, j: (i, 0)),
                  pl.BlockSpec((1, n, tn), lambda bi, i, j: (bi, 0, j)),
                  pl.BlockSpec((1, n, tn), lambda bi, i, j: (bi, 0, j))],
        out_specs=pl.BlockSpec((1, tm, tn), lambda bi, i, j: (bi, i, j)),
        out_shape=jax.ShapeDtypeStruct((b, n, e), BF16),
        compiler_params=_cparams("parallel", "parallel", "parallel"),
        name="pos_dft",
    )(cn, sn, a, bm)


def _fnet_out_kernel(y_ref, z_ref, x_ref, w_ref, mod_ref, nf_ref, out_ref, *, d, final):
    y = y_ref[0].astype(F32) * _silu(z_ref[0].astype(F32))
    yo = _dot(y.astype(BF16), w_ref[...])
    xn = x_ref[0] + mod_ref[0][:, 2 * d:3 * d] * yo
    if final:
        xn = xn * lax.rsqrt(jnp.mean(xn * xn, axis=-1, keepdims=True) + EPS) * nf_ref[...]
    out_ref[0] = xn


def _fnet_out(y, uz, x, w, mod, nf, *, tm, final):
    b, n, d = x.shape
    e = y.shape[2]
    per_batch = mod.shape[0] != 1
    mod_map = (lambda bi, t: (bi, 0, 0)) if per_batch else (lambda bi, t: (0, 0, 0))
    tok = lambda bi, t: (bi, t, 0)
    return pl.pallas_call(
        functools.partial(_fnet_out_kernel, d=d, final=final),
        grid=(b, n // tm),
        in_specs=[pl.BlockSpec((1, tm, e), tok),
                  pl.BlockSpec((1, tm, e), lambda bi, t: (bi, t, 1)),
                  pl.BlockSpec((1, tm, d), tok),
                  pl.BlockSpec((e, d), lambda bi, t: (0, 0)),
                  pl.BlockSpec((1, 1, 3 * d), mod_map),
                  pl.BlockSpec((1, d), lambda bi, t: (0, 0))],
        out_specs=pl.BlockSpec((1, tm, d), tok),
        out_shape=jax.ShapeDtypeStruct((b, n, d), F32),
        compiler_params=_cparams("parallel", "parallel"),
        name="fnet_out",
    )(y, uz, x, w, mod, nf)


def _dft_mats(n, norm):
    k = lax.broadcasted_iota(jnp.int32, (n, n), 0)
    m = lax.broadcasted_iota(jnp.int32, (n, n), 1)
    ang = ((k * m) % n).astype(F32) * (2.0 * np.pi / n)
    return (jnp.cos(ang) * norm).astype(BF16), (jnp.sin(ang) * norm).astype(BF16)


def _mlstm_layer(xl, xc, mod_l, mod_c, gain, p, *, update_ctx):
    heads = MLSTM_HEADS
    n = xl.shape[1]
    nc = xc.shape[1]
    w_in = p["w_in"].astype(BF16)
    e = w_in.shape[1] // 3
    dh = e // heads
    cw = p["conv_w"].reshape(9, e)
    cb = p["conv_b"].reshape(1, e)
    wq = p["w_q"].astype(BF16)
    wk = p["w_k"].astype(BF16)
    wkt = jnp.swapaxes(p["w_k"], 1, 2).astype(BF16)
    wif = p["w_if"].reshape(2, 3, heads, dh, 2, heads)
    wg = jnp.transpose(wif, (2, 1, 3, 4, 0, 5)).reshape(heads, 3, dh, 4 * heads)
    wg = jnp.pad(wg, ((0, 0), (0, 0), (0, 0), (0, GATE_LANES - 4 * heads))).astype(BF16)
    bias = jnp.concatenate([p["b_i"].reshape(-1), p["b_f"].reshape(-1)])
    bias = jnp.pad(bias, (0, GATE_LANES - 4 * heads)).reshape(1, GATE_LANES)
    lnw = p["ln_w"].reshape(1, e)
    skip = p["skip"].reshape(1, e)
    w_out = p["w_out"].astype(BF16)

    def branch(x, mod, tm, img_w, has_vert):
        a = _norm_matmul(x, mod, gain, w_in, tm=tm, tn=2048)
        xconv, q, k, kt, gates = _conv_qk(a, cw, cb, wq, wk, wkt, wg, tm=tm, img_w=img_w, has_vert=has_vert)
        gcol, bcol, brow = _gate_fin(gates, bias, heads=heads)
        return dict(a=a, xconv=xconv, q=q, k=k, kt=kt, gcol=gcol, bcol=bcol, brow=brow)

    lat = branch(xl, mod_l, min(1024, n), GRID_W, True)
    ctx = branch(xc, mod_c, nc, nc, False)
    outs = _scan(lat, ctx, heads=heads, emit_ctx=update_ctx)
    xl = _mlstm_out(outs[0], outs[1], lat["a"], lat["xconv"], xl, lnw, skip, w_out, mod_l,
                    tm=min(256, n), heads=heads)
    if update_ctx:
        xc = _mlstm_out(outs[2], outs[3], ctx["a"], ctx["xconv"], xc, lnw, skip, w_out, mod_c,
                        tm=nc, heads=heads)
    return xl, xc


def _fourier_layer(xl, xc, mod_l, mod_c, gain, p, nf, *, update_ctx, final):
    groups = FOURIER_GROUPS
    n = xl.shape[1]
    nc = xc.shape[1]
    w_in = p["w_in"].astype(BF16)
    w_out = p["w_out"].astype(BF16)
    e = w_in.shape[1] // 2
    gd = e // groups
    cc, sc = _dft_mats(gd, gd ** -0.5)
    wc = jnp.concatenate([cc, sc], axis=1)

    def branch(x, mod, tm, fin):
        m = x.shape[1]
        uz = _norm_matmul(x, mod, gain, w_in, tm=tm, tn=2048)
        a, bm = _chan_dft(uz, wc, tm=tm, groups=groups)
        cn, sn = _dft_mats(m, m ** -0.5)
        y = _pos_dft(cn, sn, a, bm, tm=min(512, m), tn=min(512, e))
        return _fnet_out(y, uz, x, w_out, mod, nf, tm=min(512, m), final=fin)

    xl = branch(xl, mod_l, min(1024, n), final)
    if update_ctx:
        xc = branch(xc, mod_c, nc, False)
    return xl, xc


def kernel(x, c, ctx, c_ctx, norm_g, w_ada, b_ada, m_w_in, m_conv_w, m_conv_b, m_w_q, m_w_k, m_w_if, m_b_i, m_b_f,
           m_ln_w, m_skip, m_w_out, f_w_in, f_w_out, norm_f):
    b, n, d = x.shape
    depth = norm_g.shape[0]
    assert depth % N_MIXERS == 0, "the final norm is fused into the last Fourier layer"
    last_mlstm = ((depth - 1) // N_MIXERS) * N_MIXERS
    rows = -(-(b + 1) // 8) * 8
    cc = jnp.concatenate([c, c_ctx[None, :], jnp.zeros((rows - b - 1, d), F32)], axis=0)
    mods = _ada(cc, w_ada, b_ada)
    nf = norm_f.reshape(1, d)
    xl, xc = x, ctx
    for i in range(depth):
        j = i // N_MIXERS
        update_ctx = i < last_mlstm
        mod_l = mods[i, :b][:, None, :]
        mod_c = mods[i, b:b + 1][:, None, :]
        gain = norm_g[i].reshape(1, d)
        if i % N_MIXERS == 0:
            p = dict(w_in=m_w_in[j], conv_w=m_conv_w[j], conv_b=m_conv_b[j], w_q=m_w_q[j], w_k=m_w_k[j],
                     w_if=m_w_if[j], b_i=m_b_i[j], b_f=m_b_f[j], ln_w=m_ln_w[j], skip=m_skip[j], w_out=m_w_out[j])
            xl, xc = _mlstm_layer(xl, xc, mod_l, mod_c, gain, p, update_ctx=update_ctx)
        else:
            p = dict(w_in=f_w_in[j], w_out=f_w_out[j])
            xl, xc = _fourier_layer(xl, xc, mod_l, mod_c, gain, p, nf, update_ctx=update_ctx,
                                    final=(i == depth - 1))
    return xl
```

```python
import functools

import numpy as np
import jax
import jax.numpy as jnp
from jax import lax
from jax.experimental import pallas as pl
from jax.experimental.pallas import tpu as pltpu

F32 = jnp.float32
BF16 = jnp.bfloat16

GRID_W = 64
N_MIXERS = 2
MLSTM_HEADS = 4
FOURIER_GROUPS = 8
EPS = 1e-6
SCAN_CHUNK = 256
GATE_LANES = 128
FFT_SLABS = 4
VMEM_LIMIT = 56 * 1024 * 1024


def _cparams(*sem):
    return pltpu.CompilerParams(dimension_semantics=sem, vmem_limit_bytes=VMEM_LIMIT)


def _sigmoid(v):
    return 1.0 / (1.0 + jnp.exp(-v))


def _silu(v):
    return v * _sigmoid(v)


def _dot(a, b):
    return jnp.dot(a, b, preferred_element_type=F32)


def _dot_nt(a, b):
    return lax.dot_general(a, b, (((1,), (1,)), ((), ())), preferred_element_type=F32)


def _ada_kernel(c_ref, w_ref, b_ref, o_ref):
    s = _silu(c_ref[...])
    o_ref[0] = _dot(s.astype(BF16), w_ref[0].astype(BF16)) + b_ref[0]


def _ada(cc, w_ada, b_ada):
    depth, d, d3 = w_ada.shape
    r = cc.shape[0]
    tn = 512
    return pl.pallas_call(
        _ada_kernel,
        grid=(depth, d3 // tn),
        in_specs=[pl.BlockSpec((r, d), lambda i, j: (0, 0)),
                  pl.BlockSpec((1, d, tn), lambda i, j: (i, 0, j)),
                  pl.BlockSpec((1, 1, tn), lambda i, j: (i, 0, j))],
        out_specs=pl.BlockSpec((1, r, tn), lambda i, j: (i, 0, j)),
        out_shape=jax.ShapeDtypeStruct((depth, r, d3), F32),
        compiler_params=_cparams("parallel", "parallel"),
        name="ada",
    )(cc, w_ada, b_ada.reshape(depth, 1, d3))


def _norm_matmul_kernel(x_ref, mod_ref, g_ref, w_ref, o_ref, *, d):
    x = x_ref[0]
    y = x * lax.rsqrt(jnp.mean(x * x, axis=-1, keepdims=True) + EPS) * g_ref[...]
    mod = mod_ref[0]
    h = y * (1.0 + mod[:, d:2 * d]) + mod[:, :d]
    o_ref[0] = _dot(h.astype(BF16), w_ref[...]).astype(o_ref.dtype)


def _norm_matmul(x, mod, gain, w, *, tm, tn):
    b, n, d = x.shape
    no = w.shape[1]
    per_batch = mod.shape[0] != 1
    mod_map = (lambda j, bi, t: (bi, 0, 0)) if per_batch else (lambda j, bi, t: (0, 0, 0))
    return pl.pallas_call(
        functools.partial(_norm_matmul_kernel, d=d),
        grid=(no // tn, b, n // tm),
        in_specs=[pl.BlockSpec((1, tm, d), lambda j, bi, t: (bi, t, 0)),
                  pl.BlockSpec((1, 1, 3 * d), mod_map),
                  pl.BlockSpec((1, d), lambda j, bi, t: (0, 0)),
                  pl.BlockSpec((d, tn), lambda j, bi, t: (0, j))],
        out_specs=pl.BlockSpec((1, tm, tn), lambda j, bi, t: (bi, t, j)),
        out_shape=jax.ShapeDtypeStruct((b, n, no), BF16),
        compiler_params=_cparams("parallel", "parallel", "parallel"),
        name="norm_matmul",
    )(x, mod, gain, w)


def _conv_qk_kernel(xc_ref, xp_ref, xn_ref, cw_ref, cb_ref, wq_ref, wk_ref, wkt_ref, wg_ref,
                    xconv_ref, q_ref, k_ref, kt_ref, gates_ref, *, tm, img_w, has_vert, n_tiles, scale):
    t = pl.program_id(1)
    h = pl.program_id(2)
    xcb = xc_ref[0]
    xc = xcb.astype(F32)
    if has_vert:
        xp = jnp.where(t > 0, xp_ref[0].astype(F32), 0.0)
        xn = jnp.where(t < n_tiles - 1, xn_ref[0].astype(F32), 0.0)
        ext = jnp.concatenate([xp, xc, xn], axis=0)
        off = img_w
        taps = (-1, 0, 1)
    else:
        ext = xc
        off = 0
        taps = (0,)
    rows = ext.shape[0]
    col = lax.broadcasted_iota(jnp.int32, (rows, 1), 0) % img_w
    left = jnp.where(col == 0, 0.0, pltpu.roll(ext, 1, 0))
    right = jnp.where(col == img_w - 1, 0.0, pltpu.roll(ext, rows - 1, 0))
    acc = None
    for dr in taps:
        s = off + dr * img_w
        r = 3 * (dr + 1)
        term = (cw_ref[r:r + 1, :] * left[s:s + tm] + cw_ref[r + 1:r + 2, :] * ext[s:s + tm]
                + cw_ref[r + 2:r + 3, :] * right[s:s + tm])
        acc = term if acc is None else acc + term
    xv = _silu(acc + cb_ref[...]).astype(BF16)
    xconv_ref[0] = xv
    qb = _dot(xv, wq_ref[0]).astype(BF16)
    kb = (_dot(xv, wk_ref[0]) * scale).astype(BF16)
    q_ref[0] = qb
    k_ref[0] = kb
    kt_ref[0] = (_dot_nt(wkt_ref[0], xv) * scale).astype(BF16)
    g = _dot(qb, wg_ref[0, 0]) + _dot(kb, wg_ref[0, 1]) + _dot(xcb, wg_ref[0, 2])

    @pl.when(h == 0)
    def _():
        gates_ref[0] = g

    @pl.when(h > 0)
    def _():
        gates_ref[0] += g


def _conv_qk(a, cw, cb, wq, wk, wkt, wg, *, tm, img_w, has_vert):
    b, n, _ = a.shape
    heads, dh, _ = wq.shape
    e = heads * dh
    n_tiles = n // tm
    hb = img_w if has_vert else 16
    per = tm // hb
    last = n // hb - 1
    kern = functools.partial(_conv_qk_kernel, tm=tm, img_w=img_w, has_vert=has_vert, n_tiles=n_tiles,
                             scale=float(dh) ** -0.5)
    tok = lambda bi, t, h: (bi, t, h)
    return pl.pallas_call(
        kern,
        grid=(b, n_tiles, heads),
        in_specs=[pl.BlockSpec((1, tm, dh), tok),
                  pl.BlockSpec((1, hb, dh), lambda bi, t, h: (bi, jnp.maximum(t * per - 1, 0), h)),
                  pl.BlockSpec((1, hb, dh), lambda bi, t, h: (bi, jnp.minimum((t + 1) * per, last), h)),
                  pl.BlockSpec((9, dh), lambda bi, t, h: (0, h)),
                  pl.BlockSpec((1, dh), lambda bi, t, h: (0, h)),
                  pl.BlockSpec((1, dh, dh), lambda bi, t, h: (h, 0, 0)),
                  pl.BlockSpec((1, dh, dh), lambda bi, t, h: (h, 0, 0)),
                  pl.BlockSpec((1, dh, dh), lambda bi, t, h: (h, 0, 0)),
                  pl.BlockSpec((1, 3, dh, GATE_LANES), lambda bi, t, h: (h, 0, 0, 0))],
        out_specs=[pl.BlockSpec((1, tm, dh), tok),
                   pl.BlockSpec((1, tm, dh), tok),
                   pl.BlockSpec((1, tm, dh), tok),
                   pl.BlockSpec((1, dh, tm), lambda bi, t, h: (bi, h, t)),
                   pl.BlockSpec((1, tm, GATE_LANES), lambda bi, t, h: (bi, t, 0))],
        out_shape=[jax.ShapeDtypeStruct((b, n, e), BF16),
                   jax.ShapeDtypeStruct((b, n, e), BF16),
                   jax.ShapeDtypeStruct((b, n, e), BF16),
                   jax.ShapeDtypeStruct((b, e, n), BF16),
                   jax.ShapeDtypeStruct((b, n, GATE_LANES), F32)],
        compiler_params=_cparams("parallel", "parallel", "arbitrary"),
        name="conv_qk",
    )(a, a, a, cw, cb, wq, wk, wkt, wg)


def _split3(a):
    hi = a.astype(BF16)
    r1 = a - hi.astype(F32)
    mid = r1.astype(BF16)
    lo = (r1 - mid.astype(F32)).astype(BF16)
    return hi, mid, lo


def _gate_fin_kernel(g_ref, bias_ref, gcol_ref, bcol_ref, brow_ref, *, L, heads):
    nd = 2 * heads
    pre = g_ref[0] + bias_ref[...]
    lf = jnp.minimum(pre, 0.0) - jnp.log(1.0 + jnp.exp(-jnp.abs(pre)))
    row = lax.broadcasted_iota(jnp.int32, (L, L), 0)
    col = lax.broadcasted_iota(jnp.int32, (L, L), 1)
    tri_l = (col <= row).astype(BF16)
    tri_u = (col >= row).astype(BF16)
    parts = _split3(lf)
    g_l = _dot(tri_l, parts[0]) + _dot(tri_l, parts[1]) + _dot(tri_l, parts[2])
    g_u = _dot(tri_u, parts[0]) + _dot(tri_u, parts[1]) + _dot(tri_u, parts[2])
    lane = lax.broadcasted_iota(jnp.int32, (L, GATE_LANES), 1)
    g_sel = jnp.where(lane < nd + heads, g_l, g_u)
    g = pltpu.roll(g_sel, GATE_LANES - nd, 1)
    bc = pre - g
    gcol_ref[0] = g
    bcol_ref[0] = bc
    brow_ref[0] = jnp.transpose(bc)[0:8, :]


def _gate_fin(gates, bias, *, heads):
    b, n, _ = gates.shape
    L = SCAN_CHUNK
    assert 2 * heads == 8
    blk = pl.BlockSpec((1, L, GATE_LANES), lambda bi, c: (bi, c, 0))
    return pl.pallas_call(
        functools.partial(_gate_fin_kernel, L=L, heads=heads),
        grid=(b, n // L),
        in_specs=[blk, pl.BlockSpec((1, GATE_LANES), lambda bi, c: (0, 0))],
        out_specs=[blk, blk, pl.BlockSpec((1, 8, L), lambda bi, c: (bi, 0, c))],
        out_shape=[jax.ShapeDtypeStruct((b, n, GATE_LANES), F32),
                   jax.ShapeDtypeStruct((b, n, GATE_LANES), F32),
                   jax.ShapeDtypeStruct((b, 8, n), F32)],
        compiler_params=_cparams("parallel", "parallel"),
        name="gate_fin",
    )(gates, bias)


def _chain_step(q_ref, k_ref, kt_ref, v_ref, gc_ref, bc_ref, br_ref, out_ref, ct_ref, n_ref, m_ref,
                *, slot, lane_idx, lower, L):
    q = q_ref[0]
    kt = kt_ref[0]
    v = v_ref[0]
    lane = lax.broadcasted_iota(jnp.int32, (L, GATE_LANES), 1)
    gcol = jnp.sum(jnp.where(lane == lane_idx, gc_ref[0], 0.0), axis=1, keepdims=True)
    bcol = jnp.sum(jnp.where(lane == lane_idx, bc_ref[0], 0.0), axis=1, keepdims=True)
    brow = br_ref[0, pl.ds(lane_idx, 1), :]
    m_prev = m_ref[slot][:, 0:1]

    row = lax.broadcasted_iota(jnp.int32, (L, L), 0)
    col = lax.broadcasted_iota(jnp.int32, (L, L), 1)
    mask = (col <= row) if lower else (col >= row)
    logw = jnp.where(mask, gcol + brow, -jnp.inf)
    log_prev = gcol + m_prev
    m_new = jnp.maximum(log_prev, jnp.max(logw, axis=1, keepdims=True))
    w_intra = jnp.exp(logw - m_new)
    w_prev = jnp.exp(log_prev - m_new)

    if out_ref is not None:
        p = _dot(q, kt) * w_intra
        n_row = n_ref[slot]
        qn = jnp.sum(q.astype(F32) * n_row, axis=1, keepdims=True)
        den = jnp.sum(p, axis=1, keepdims=True) + w_prev * qn
        num = _dot(p.astype(BF16), v) + w_prev * _dot(q, ct_ref[slot].astype(BF16))
        inv = 1.0 / jnp.maximum(jnp.abs(den), jnp.exp(-m_new))
        out_ref[0] = (num * inv).astype(out_ref.dtype)

    rowv = lax.broadcasted_iota(jnp.int32, (L, 1), 0)
    last = (rowv == (L - 1 if lower else 0))
    g_last = jnp.sum(jnp.where(last, gcol, 0.0), axis=0, keepdims=True)
    m_end = jnp.sum(jnp.where(last, m_new, 0.0), axis=0, keepdims=True)
    w_src = jnp.exp(bcol + g_last - m_end)
    decay = jnp.exp(g_last + m_prev - m_end)
    vw = (w_src * v.astype(F32)).astype(BF16)
    ct_ref[slot] = decay * ct_ref[slot] + _dot(kt, vw)
    n_ref[slot] = decay * n_ref[slot] + jnp.sum(w_src * k_ref[0].astype(F32), axis=0, keepdims=True)
    m_ref[slot] = jnp.broadcast_to(m_end, m_ref.shape[1:])


def _scan_kernel(*refs, L, heads, emit_ctx):
    ctx_in = refs[0:7]
    fwd_in = refs[7:14]
    rev_in = refs[14:21]
    hf_ref, hr_ref = refs[21:23]
    if emit_ctx:
        hcf_ref, hcr_ref = refs[23:25]
        ct_ref, n_ref, m_ref = refs[25:28]
    else:
        hcf_ref = hcr_ref = None
        ct_ref, n_ref, m_ref = refs[23:26]
    h = pl.program_id(1)
    s = pl.program_id(2)
    step = functools.partial(_chain_step, ct_ref=ct_ref, n_ref=n_ref, m_ref=m_ref, L=L)

    @pl.when(s == 0)
    def _():
        ct_ref[...] = jnp.zeros_like(ct_ref)
        n_ref[...] = jnp.zeros_like(n_ref)
        m_ref[...] = jnp.zeros_like(m_ref)
        step(*ctx_in, hcf_ref, slot=0, lane_idx=h, lower=True)
        step(*ctx_in, hcr_ref, slot=1, lane_idx=heads + h, lower=False)

    @pl.when(s > 0)
    def _():
        step(*fwd_in, hf_ref, slot=0, lane_idx=h, lower=True)
        step(*rev_in, hr_ref, slot=1, lane_idx=heads + h, lower=False)


def _scan(lat, ctx, *, heads, emit_ctx):
    b, n, e = lat["q"].shape
    nc = ctx["q"].shape[1]
    L = SCAN_CHUNK
    assert nc == L, "the context prefix is processed as exactly one chunk"
    dh = e // heads
    n_lat = n // L

    def specs(imap):
        tok = lambda bi, h, s: (bi, imap(s), h)
        return [pl.BlockSpec((1, L, dh), tok),
                pl.BlockSpec((1, L, dh), tok),
                pl.BlockSpec((1, dh, L), lambda bi, h, s: (bi, h, imap(s))),
                pl.BlockSpec((1, L, dh), tok),
                pl.BlockSpec((1, L, GATE_LANES), lambda bi, h, s: (bi, imap(s), 0)),
                pl.BlockSpec((1, L, GATE_LANES), lambda bi, h, s: (bi, imap(s), 0)),
                pl.BlockSpec((1, 8, L), lambda bi, h, s: (bi, 0, imap(s)))]

    def args(d):
        return [d["q"], d["k"], d["kt"], d["a"], d["gcol"], d["bcol"], d["brow"]]

    ctx_map = lambda s: 0
    fwd_map = lambda s: jnp.maximum(s - 1, 0)
    rev_map = lambda s: n_lat - jnp.maximum(s, 1)
    out_specs = [pl.BlockSpec((1, L, dh), lambda bi, h, s: (bi, fwd_map(s), h)),
                 pl.BlockSpec((1, L, dh), lambda bi, h, s: (bi, rev_map(s), h))]
    out_shape = [jax.ShapeDtypeStruct((b, n, e), BF16), jax.ShapeDtypeStruct((b, n, e), BF16)]
    if emit_ctx:
        out_specs += [pl.BlockSpec((1, L, dh), lambda bi, h, s: (bi, 0, h))] * 2
        out_shape += [jax.ShapeDtypeStruct((b, nc, e), BF16)] * 2
    return pl.pallas_call(
        functools.partial(_scan_kernel, L=L, heads=heads, emit_ctx=emit_ctx),
        grid=(b, heads, n_lat + 1),
        in_specs=specs(ctx_map) + specs(fwd_map) + specs(rev_map),
        out_specs=out_specs,
        out_shape=out_shape,
        scratch_shapes=[pltpu.VMEM((2, dh, dh), F32),
                        pltpu.VMEM((2, 1, dh), F32),
                        pltpu.VMEM((2, 1, GATE_LANES), F32)],
        compiler_params=_cparams("parallel", "parallel", "arbitrary"),
        name="mlstm_scan",
    )(*(args(ctx) + args(lat) + args(lat)))


def _mlstm_out_kernel(hf_ref, hr_ref, o_ref, z_ref, xc_ref, x_ref, lnw_ref, skip_ref, w_ref, mod_ref, out_ref,
                      *, d, dh, heads):
    hs = hf_ref[0].astype(F32) + hr_ref[0].astype(F32)
    parts = []
    for hh in range(heads):
        seg = hs[:, hh * dh:(hh + 1) * dh]
        cen = seg - jnp.mean(seg, axis=-1, keepdims=True)
        var = jnp.mean(cen * cen, axis=-1, keepdims=True)
        parts.append(cen * lax.rsqrt(var + EPS))
    hn = jnp.concatenate(parts, axis=1) * lnw_ref[...]
    hn = _sigmoid(o_ref[0].astype(F32)) * hn
    y = (hn + skip_ref[...] * xc_ref[0].astype(F32)) * _silu(z_ref[0].astype(F32))
    yo = _dot(y.astype(BF16), w_ref[...])
    out_ref[0] = x_ref[0] + mod_ref[0][:, 2 * d:3 * d] * yo


def _mlstm_out(hf, hr, a, xconv, x, lnw, skip, w, mod, *, tm, heads):
    b, n, d = x.shape
    e = hf.shape[2]
    per_batch = mod.shape[0] != 1
    mod_map = (lambda bi, t: (bi, 0, 0)) if per_batch else (lambda bi, t: (0, 0, 0))
    tok = lambda bi, t: (bi, t, 0)
    vec = pl.BlockSpec((1, e), lambda bi, t: (0, 0))
    return pl.pallas_call(
        functools.partial(_mlstm_out_kernel, d=d, dh=e // heads, heads=heads),
        grid=(b, n // tm),
        in_specs=[pl.BlockSpec((1, tm, e), tok),
                  pl.BlockSpec((1, tm, e), tok),
                  pl.BlockSpec((1, tm, e), lambda bi, t: (bi, t, 1)),
                  pl.BlockSpec((1, tm, e), lambda bi, t: (bi, t, 2)),
                  pl.BlockSpec((1, tm, e), tok),
                  pl.BlockSpec((1, tm, d), tok),
                  vec, vec,
                  pl.BlockSpec((e, d), lambda bi, t: (0, 0)),
                  pl.BlockSpec((1, 1, 3 * d), mod_map)],
        out_specs=pl.BlockSpec((1, tm, d), tok),
        out_shape=jax.ShapeDtypeStruct((b, n, d), F32),
        compiler_params=_cparams("parallel", "parallel"),
        name="mlstm_out",
    )(hf, hr, a, a, xconv, x, lnw, skip, w, mod)


def _chan_dft_kernel(u_ref, w_ref, a_ref, b_ref, *, groups, gd):
    for g in range(groups):
        r = _dot(u_ref[0, :, g * gd:(g + 1) * gd], w_ref[...])
        a_ref[0, :, g * gd:(g + 1) * gd] = r[:, :gd].astype(BF16)
        b_ref[0, :, g * gd:(g + 1) * gd] = r[:, gd:].astype(BF16)


def _chan_dft(uz, wc, *, tm, groups):
    b, n, e2 = uz.shape
    e = e2 // 2
    gd = e // groups
    tok = lambda bi, t: (bi, t, 0)
    return pl.pallas_call(
        functools.partial(_chan_dft_kernel, groups=groups, gd=gd),
        grid=(b, n // tm),
        in_specs=[pl.BlockSpec((1, tm, e), tok), pl.BlockSpec((gd, 2 * gd), lambda bi, t: (0, 0))],
        out_specs=[pl.BlockSpec((1, tm, e), tok)] * 2,
        out_shape=[jax.ShapeDtypeStruct((b, n, e), BF16)] * 2,
        compiler_params=_cparams("parallel", "parallel"),
        name="chan_dft",
    )(uz, wc)


def _pos_dft_kernel(cn_ref, sn_ref, a_ref, b_ref, y_ref):
    y_ref[0] = (_dot(cn_ref[...], a_ref[0]) - _dot(sn_ref[...], b_ref[0])).astype(y_ref.dtype)


def _pos_dft(cn, sn, a, bm, *, tm, tn):
    b, n, e = a.shape
    return pl.pallas_call(
        _pos_dft_kernel,
        grid=(b, n // tm, e // tn),
        in_specs=[pl.BlockSpec((tm, n), lambda bi, i, j: (i, 0)),
                  pl.BlockSpec((tm, n), lambda bi, i, j: (i, 0)),
                  pl.BlockSpec((1, n, tn), lambda bi, i, j: (bi, 0, j)),
                  pl.BlockSpec((1, n, tn), lambda bi, i, j: (bi, 0, j))],
        out_specs=pl.BlockSpec((1, tm, tn), lambda bi, i, j: (bi, i, j)),
        out_shape=jax.ShapeDtypeStruct((b, n, e), BF16),
        compiler_params=_cparams("parallel", "parallel", "parallel"),
        name="pos_dft",
    )(cn, sn, a, bm)


def _small_dft(zs):
    n = len(zs)
    if n == 1:
        return zs
    even = _small_dft(zs[0::2])
    odd = _small_dft(zs[1::2])
    out = [None] * n
    for k in range(n // 2):
        o_re, o_im = odd[k]
        if k == 0:
            t_re, t_im = o_re, o_im
        elif 4 * k == n:
            t_re, t_im = o_im, -o_re
        else:
            c = float(np.cos(2.0 * np.pi * k / n))
            s = float(np.sin(2.0 * np.pi * k / n))
            t_re, t_im = c * o_re + s * o_im, c * o_im - s * o_re
        e_re, e_im = even[k]
        out[k] = (e_re + t_re, e_im + t_im)
        out[k + n // 2] = (e_re - t_re, e_im - t_im)
    return out


def _fft_a_kernel(u_ref, wc_ref, tc_ref, ts_ref, gr_ref, gi_ref, *, n1, gd):
    zs = []
    for a in range(n1):
        r = _dot(u_ref[0, a], wc_ref[...])
        zs.append((r[:, :gd], r[:, gd:]))
    gs = _small_dft(zs)
    for k1 in range(n1):
        re, im = gs[k1]
        if k1 > 0:
            c = tc_ref[k1]
            s = ts_ref[k1]
            re, im = re * c + im * s, im * c - re * s
        gr_ref[0, k1] = re.astype(BF16)
        gi_ref[0, k1] = im.astype(BF16)


def _fft_a(uz, wc, tc, ts, *, n1, groups, tr):
    b, n, e2 = uz.shape
    e = e2 // 2
    gd = e // groups
    n2 = n // n1
    blk = pl.BlockSpec((1, n1, tr, gd), lambda bi, r, g: (bi, 0, r, g))
    tw = pl.BlockSpec((n1, tr, gd), lambda bi, r, g: (0, r, 0))
    return pl.pallas_call(
        functools.partial(_fft_a_kernel, n1=n1, gd=gd),
        grid=(b, n2 // tr, groups),
        in_specs=[blk, pl.BlockSpec((gd, 2 * gd), lambda bi, r, g: (0, 0)), tw, tw],
        out_specs=[blk, blk],
        out_shape=[jax.ShapeDtypeStruct((b, n1, n2, e), BF16)] * 2,
        compiler_params=_cparams("parallel", "parallel", "parallel"),
        name="fft_a",
    )(uz.reshape(b, n1, n2, e2), wc, tc, ts)


def _fft_b_kernel(wc_ref, ws_ref, gr_ref, gi_ref, y_ref, scr_ref, *, n1, n2):
    for k1 in range(n1):
        yk = _dot(wc_ref[...], gr_ref[0, k1]) + _dot(ws_ref[...], gi_ref[0, k1])
        for c in range(scr_ref.shape[0]):
            scr_ref[c, pl.ds(k1, n2, stride=n1), :] = yk[:, c * 128:(c + 1) * 128]
    for c in range(scr_ref.shape[0]):
        y_ref[0, :, c * 128:(c + 1) * 128] = scr_ref[c].astype(y_ref.dtype)


def _fft_b(w2c, w2s, gr, gi, *, tn):
    b, n1, n2, e = gr.shape
    blk = pl.BlockSpec((1, n1, n2, tn), lambda bi, j: (bi, 0, 0, j))
    wsp = pl.BlockSpec((n2, n2), lambda bi, j: (0, 0))
    return pl.pallas_call(
        functools.partial(_fft_b_kernel, n1=n1, n2=n2),
        grid=(b, e // tn),
        in_specs=[wsp, wsp, blk, blk],
        out_specs=pl.BlockSpec((1, n1 * n2, tn), lambda bi, j: (bi, 0, j)),
        out_shape=jax.ShapeDtypeStruct((b, n1 * n2, e), BF16),
        scratch_shapes=[pltpu.VMEM((tn // 128, n1 * n2, 128), F32)],
        compiler_params=_cparams("parallel", "parallel"),
        name="fft_b",
    )(w2c, w2s, gr, gi)


def _fnet_out_kernel(y_ref, z_ref, x_ref, w_ref, mod_ref, nf_ref, out_ref, *, d, final):
    y = y_ref[0].astype(F32) * _silu(z_ref[0].astype(F32))
    yo = _dot(y.astype(BF16), w_ref[...])
    xn = x_ref[0] + mod_ref[0][:, 2 * d:3 * d] * yo
    if final:
        xn = xn * lax.rsqrt(jnp.mean(xn * xn, axis=-1, keepdims=True) + EPS) * nf_ref[...]
    out_ref[0] = xn


def _fnet_out(y, uz, x, w, mod, nf, *, tm, final):
    b, n, d = x.shape
    e = y.shape[2]
    per_batch = mod.shape[0] != 1
    mod_map = (lambda bi, t: (bi, 0, 0)) if per_batch else (lambda bi, t: (0, 0, 0))
    tok = lambda bi, t: (bi, t, 0)
    return pl.pallas_call(
        functools.partial(_fnet_out_kernel, d=d, final=final),
        grid=(b, n // tm),
        in_specs=[pl.BlockSpec((1, tm, e), tok),
                  pl.BlockSpec((1, tm, e), lambda bi, t: (bi, t, 1)),
                  pl.BlockSpec((1, tm, d), tok),
                  pl.BlockSpec((e, d), lambda bi, t: (0, 0)),
                  pl.BlockSpec((1, 1, 3 * d), mod_map),
                  pl.BlockSpec((1, d), lambda bi, t: (0, 0))],
        out_specs=pl.BlockSpec((1, tm, d), tok),
        out_shape=jax.ShapeDtypeStruct((b, n, d), F32),
        compiler_params=_cparams("parallel", "parallel"),
        name="fnet_out",
    )(y, uz, x, w, mod, nf)


def _dft_mats(n, norm):
    k = lax.broadcasted_iota(jnp.int32, (n, n), 0)
    m = lax.broadcasted_iota(jnp.int32, (n, n), 1)
    ang = ((k * m) % n).astype(F32) * (2.0 * np.pi / n)
    return (jnp.cos(ang) * norm).astype(BF16), (jnp.sin(ang) * norm).astype(BF16)


def _mlstm_layer(xl, xc, mod_l, mod_c, gain, p, *, update_ctx):
    heads = MLSTM_HEADS
    n = xl.shape[1]
    nc = xc.shape[1]
    w_in = p["w_in"].astype(BF16)
    e = w_in.shape[1] // 3
    dh = e // heads
    cw = p["conv_w"].reshape(9, e)
    cb = p["conv_b"].reshape(1, e)
    wq = p["w_q"].astype(BF16)
    wk = p["w_k"].astype(BF16)
    wkt = jnp.swapaxes(p["w_k"], 1, 2).astype(BF16)
    wif = p["w_if"].reshape(2, 3, heads, dh, 2, heads)
    wg = jnp.transpose(wif, (2, 1, 3, 4, 0, 5)).reshape(heads, 3, dh, 4 * heads)
    wg = jnp.pad(wg, ((0, 0), (0, 0), (0, 0), (0, GATE_LANES - 4 * heads))).astype(BF16)
    bias = jnp.concatenate([p["b_i"].reshape(-1), p["b_f"].reshape(-1)])
    bias = jnp.pad(bias, (0, GATE_LANES - 4 * heads)).reshape(1, GATE_LANES)
    lnw = p["ln_w"].reshape(1, e)
    skip = p["skip"].reshape(1, e)
    w_out = p["w_out"].astype(BF16)

    def branch(x, mod, tm, img_w, has_vert):
        a = _norm_matmul(x, mod, gain, w_in, tm=tm, tn=2048)
        xconv, q, k, kt, gates = _conv_qk(a, cw, cb, wq, wk, wkt, wg, tm=tm, img_w=img_w, has_vert=has_vert)
        gcol, bcol, brow = _gate_fin(gates, bias, heads=heads)
        return dict(a=a, xconv=xconv, q=q, k=k, kt=kt, gcol=gcol, bcol=bcol, brow=brow)

    lat = branch(xl, mod_l, min(1024, n), GRID_W, True)
    ctx = branch(xc, mod_c, nc, nc, False)
    outs = _scan(lat, ctx, heads=heads, emit_ctx=update_ctx)
    xl = _mlstm_out(outs[0], outs[1], lat["a"], lat["xconv"], xl, lnw, skip, w_out, mod_l,
                    tm=min(256, n), heads=heads)
    if update_ctx:
        xc = _mlstm_out(outs[2], outs[3], ctx["a"], ctx["xconv"], xc, lnw, skip, w_out, mod_c,
                        tm=nc, heads=heads)
    return xl, xc


def _fourier_layer(xl, xc, mod_l, mod_c, gain, p, nf, *, update_ctx, final):
    groups = FOURIER_GROUPS
    n = xl.shape[1]
    nc = xc.shape[1]
    w_in = p["w_in"].astype(BF16)
    w_out = p["w_out"].astype(BF16)
    e = w_in.shape[1] // 2
    gd = e // groups
    cc, sc = _dft_mats(gd, gd ** -0.5)
    wc = jnp.concatenate([cc, sc], axis=1)

    def position_dft(uz, m):
        n1 = FFT_SLABS if m % (FFT_SLABS * 256) == 0 else 1
        if n1 == 1:
            a, bm = _chan_dft(uz, wc, tm=min(1024, m), groups=groups)
            cn, sn = _dft_mats(m, m ** -0.5)
            return _pos_dft(cn, sn, a, bm, tm=min(512, m), tn=min(512, e))
        n2 = m // n1
        k1 = lax.broadcasted_iota(jnp.int32, (n1, n2, gd), 0)
        p2 = lax.broadcasted_iota(jnp.int32, (n1, n2, gd), 1)
        ang = ((k1 * p2) % m).astype(F32) * (2.0 * np.pi / m)
        gr, gi = _fft_a(uz, jnp.concatenate([cc, -sc], axis=1), jnp.cos(ang), jnp.sin(ang),
                        n1=n1, groups=groups, tr=min(512, n2))
        w2c, w2s = _dft_mats(n2, m ** -0.5)
        return _fft_b(w2c, w2s, gr, gi, tn=256)

    def branch(x, mod, tm, fin):
        m = x.shape[1]
        uz = _norm_matmul(x, mod, gain, w_in, tm=tm, tn=2048)
        y = position_dft(uz, m)
        return _fnet_out(y, uz, x, w_out, mod, nf, tm=min(512, m), final=fin)

    xl = branch(xl, mod_l, min(1024, n), final)
    if update_ctx:
        xc = branch(xc, mod_c, nc, False)
    return xl, xc


def kernel(x, c, ctx, c_ctx, norm_g, w_ada, b_ada, m_w_in, m_conv_w, m_conv_b, m_w_q, m_w_k, m_w_if, m_b_i, m_b_f,
           m_ln_w, m_skip, m_w_out, f_w_in, f_w_out, norm_f):
    b, n, d = x.shape
    depth = norm_g.shape[0]
    assert depth % N_MIXERS == 0, "the final norm is fused into the last Fourier layer"
    last_mlstm = ((depth - 1) // N_MIXERS) * N_MIXERS
    rows = -(-(b + 1) // 8) * 8
    cc = jnp.concatenate([c, c_ctx[None, :], jnp.zeros((rows - b - 1, d), F32)], axis=0)
    mods = _ada(cc, w_ada, b_ada)
    nf = norm_f.reshape(1, d)
    xl, xc = x, ctx
    for i in range(depth):
        j = i // N_MIXERS
        update_ctx = i < last_mlstm
        mod_l = mods[i, :b][:, None, :]
        mod_c = mods[i, b:b + 1][:, None, :]
        gain = norm_g[i].reshape(1, d)
        if i % N_MIXERS == 0:
            p = dict(w_in=m_w_in[j], conv_w=m_conv_w[j], conv_b=m_conv_b[j], w_q=m_w_q[j], w_k=m_w_k[j],
                     w_if=m_w_if[j], b_i=m_b_i[j], b_f=m_b_f[j], ln_w=m_ln_w[j], skip=m_skip[j], w_out=m_w_out[j])
            xl, xc = _mlstm_layer(xl, xc, mod_l, mod_c, gain, p, update_ctx=update_ctx)
        else:
            p = dict(w_in=f_w_in[j], w_out=f_w_out[j])
            xl, xc = _fourier_layer(xl, xc, mod_l, mod_c, gain, p, nf, update_ctx=update_ctx,
                                    final=(i == depth - 1))
    return xl
```

```python
import functools

import numpy as np
import jax
import jax.numpy as jnp
from jax import lax
from jax.experimental import pallas as pl
from jax.experimental.pallas import tpu as pltpu

F32 = jnp.float32
BF16 = jnp.bfloat16

GRID_W = 64
N_MIXERS = 2
MLSTM_HEADS = 4
FOURIER_GROUPS = 8
EPS = 1e-6
SCAN_CHUNK = 256
CONV_SUB = 256
GATE_LANES = 128
FFT_SLABS = 4
VMEM_LIMIT = 56 * 1024 * 1024


def _cparams(*sem):
    return pltpu.CompilerParams(dimension_semantics=sem, vmem_limit_bytes=VMEM_LIMIT)


def _sigmoid(v):
    return 1.0 / (1.0 + jnp.exp(-v))


def _silu(v):
    return v * _sigmoid(v)


def _dot(a, b):
    return jnp.dot(a, b, preferred_element_type=F32)


def _dot_nt(a, b):
    return lax.dot_general(a, b, (((1,), (1,)), ((), ())), preferred_element_type=F32)


def _ada_kernel(c_ref, w_ref, b_ref, o_ref):
    s = _silu(c_ref[...])
    o_ref[0] = _dot(s.astype(BF16), w_ref[0].astype(BF16)) + b_ref[0]


def _ada(cc, w_ada, b_ada):
    depth, d, d3 = w_ada.shape
    r = cc.shape[0]
    tn = 512
    return pl.pallas_call(
        _ada_kernel,
        grid=(depth, d3 // tn),
        in_specs=[pl.BlockSpec((r, d), lambda i, j: (0, 0)),
                  pl.BlockSpec((1, d, tn), lambda i, j: (i, 0, j)),
                  pl.BlockSpec((1, 1, tn), lambda i, j: (i, 0, j))],
        out_specs=pl.BlockSpec((1, r, tn), lambda i, j: (i, 0, j)),
        out_shape=jax.ShapeDtypeStruct((depth, r, d3), F32),
        compiler_params=_cparams("parallel", "parallel"),
        name="ada",
    )(cc, w_ada, b_ada.reshape(depth, 1, d3))


_ACTS = {"none": lambda v: v, "sigmoid": _sigmoid, "silu": _silu}


def _norm_matmul_kernel(x_ref, mod_ref, g_ref, w_ref, o_ref, *, d, acts):
    x = x_ref[0]
    y = x * lax.rsqrt(jnp.mean(x * x, axis=-1, keepdims=True) + EPS) * g_ref[...]
    mod = mod_ref[0]
    h = y * (1.0 + mod[:, d:2 * d]) + mod[:, :d]
    r = _dot(h.astype(BF16), w_ref[...])
    j = pl.program_id(0)
    for idx, act in enumerate(acts):

        @pl.when(j == idx)
        def _(act=act):
            o_ref[0] = _ACTS[act](r).astype(o_ref.dtype)


def _norm_matmul(x, mod, gain, w, *, tm, tn, acts):
    b, n, d = x.shape
    no = w.shape[1]
    assert len(acts) == no // tn
    per_batch = mod.shape[0] != 1
    mod_map = (lambda j, bi, t: (bi, 0, 0)) if per_batch else (lambda j, bi, t: (0, 0, 0))
    return pl.pallas_call(
        functools.partial(_norm_matmul_kernel, d=d, acts=acts),
        grid=(no // tn, b, n // tm),
        in_specs=[pl.BlockSpec((1, tm, d), lambda j, bi, t: (bi, t, 0)),
                  pl.BlockSpec((1, 1, 3 * d), mod_map),
                  pl.BlockSpec((1, d), lambda j, bi, t: (0, 0)),
                  pl.BlockSpec((d, tn), lambda j, bi, t: (0, j))],
        out_specs=pl.BlockSpec((1, tm, tn), lambda j, bi, t: (bi, t, j)),
        out_shape=jax.ShapeDtypeStruct((b, n, no), BF16),
        compiler_params=_cparams("parallel", "parallel", "parallel"),
        name="norm_matmul",
    )(x, mod, gain, w)


def _conv_qk_kernel(xc_ref, xp_ref, xn_ref, cw_ref, cb_ref, wq_ref, wk_ref, wkt_ref, wg_ref,
                    xconv_ref, q_ref, k_ref, kt_ref, gates_ref, *, tm, img_w, has_vert, n_tiles, scale):
    t = pl.program_id(1)
    sub = min(tm, CONV_SUB)
    n_sub = tm // sub
    for c in range(n_sub):
        lo = c * sub
        xcb = xc_ref[0, lo:lo + sub, :]
        xc = xcb.astype(F32)
        if has_vert:
            if c == 0:
                top = jnp.where(t > 0, xp_ref[0].astype(F32), 0.0)
            else:
                top = xc_ref[0, lo - img_w:lo, :].astype(F32)
            if c == n_sub - 1:
                bot = jnp.where(t < n_tiles - 1, xn_ref[0].astype(F32), 0.0)
            else:
                bot = xc_ref[0, lo + sub:lo + sub + img_w, :].astype(F32)
            ext = jnp.concatenate([top, xc, bot], axis=0)
            off = img_w
            taps = (-1, 0, 1)
        else:
            ext = xc
            off = 0
            taps = (0,)
        rows = ext.shape[0]
        col = lax.broadcasted_iota(jnp.int32, (rows, 1), 0) % img_w
        left = jnp.where(col == 0, 0.0, pltpu.roll(ext, 1, 0))
        right = jnp.where(col == img_w - 1, 0.0, pltpu.roll(ext, rows - 1, 0))
        acc = None
        for dr in taps:
            s = off + dr * img_w
            r = 3 * (dr + 1)
            term = (cw_ref[r:r + 1, :] * left[s:s + sub] + cw_ref[r + 1:r + 2, :] * ext[s:s + sub]
                    + cw_ref[r + 2:r + 3, :] * right[s:s + sub])
            acc = term if acc is None else acc + term
        xv = _silu(acc + cb_ref[...]).astype(BF16)
        xconv_ref[0, lo:lo + sub, :] = xv
        qb = _dot(xv, wq_ref[0]).astype(BF16)
        kb = (_dot(xv, wk_ref[0]) * scale).astype(BF16)
        q_ref[0, lo:lo + sub, :] = qb
        k_ref[0, lo:lo + sub, :] = kb
        kt_ref[0, :, lo:lo + sub] = (_dot_nt(wkt_ref[0], xv) * scale).astype(BF16)
        gates_ref[0, 0, lo:lo + sub, :] = (_dot(qb, wg_ref[0, 0]) + _dot(kb, wg_ref[0, 1])
                                           + _dot(xcb, wg_ref[0, 2]))


def _conv_qk(a, cw, cb, wq, wk, wkt, wg, *, tm, img_w, has_vert):
    b, n, _ = a.shape
    heads, dh, _ = wq.shape
    e = heads * dh
    n_tiles = n // tm
    hb = img_w if has_vert else 16
    per = tm // hb
    last = n // hb - 1
    kern = functools.partial(_conv_qk_kernel, tm=tm, img_w=img_w, has_vert=has_vert, n_tiles=n_tiles,
                             scale=float(dh) ** -0.5)
    tok = lambda bi, t, h: (bi, t, h)
    return pl.pallas_call(
        kern,
        grid=(b, n_tiles, heads),
        in_specs=[pl.BlockSpec((1, tm, dh), tok),
                  pl.BlockSpec((1, hb, dh), lambda bi, t, h: (bi, jnp.maximum(t * per - 1, 0), h)),
                  pl.BlockSpec((1, hb, dh), lambda bi, t, h: (bi, jnp.minimum((t + 1) * per, last), h)),
                  pl.BlockSpec((9, dh), lambda bi, t, h: (0, h)),
                  pl.BlockSpec((1, dh), lambda bi, t, h: (0, h)),
                  pl.BlockSpec((1, dh, dh), lambda bi, t, h: (h, 0, 0)),
                  pl.BlockSpec((1, dh, dh), lambda bi, t, h: (h, 0, 0)),
                  pl.BlockSpec((1, dh, dh), lambda bi, t, h: (h, 0, 0)),
                  pl.BlockSpec((1, 3, dh, GATE_LANES), lambda bi, t, h: (h, 0, 0, 0))],
        out_specs=[pl.BlockSpec((1, tm, dh), tok),
                   pl.BlockSpec((1, tm, dh), tok),
                   pl.BlockSpec((1, tm, dh), tok),
                   pl.BlockSpec((1, dh, tm), lambda bi, t, h: (bi, h, t)),
                   pl.BlockSpec((1, 1, tm, GATE_LANES), lambda bi, t, h: (bi, h, t, 0))],
        out_shape=[jax.ShapeDtypeStruct((b, n, e), BF16),
                   jax.ShapeDtypeStruct((b, n, e), BF16),
                   jax.ShapeDtypeStruct((b, n, e), BF16),
                   jax.ShapeDtypeStruct((b, e, n), BF16),
                   jax.ShapeDtypeStruct((b, heads, n, GATE_LANES), F32)],
        compiler_params=_cparams("parallel", "parallel", "parallel"),
        name="conv_qk",
    )(a, a, a, cw, cb, wq, wk, wkt, wg)


def _split3(a):
    hi = a.astype(BF16)
    r1 = a - hi.astype(F32)
    mid = r1.astype(BF16)
    lo = (r1 - mid.astype(F32)).astype(BF16)
    return hi, mid, lo


def _gate_fin_kernel(g_ref, bias_ref, gcol_ref, bcol_ref, brow_ref, *, L, heads):
    nd = 2 * heads
    pre = jnp.sum(g_ref[0], axis=0) + bias_ref[...]
    lf = jnp.minimum(pre, 0.0) - jnp.log(1.0 + jnp.exp(-jnp.abs(pre)))
    row = lax.broadcasted_iota(jnp.int32, (L, L), 0)
    col = lax.broadcasted_iota(jnp.int32, (L, L), 1)
    tri_l = (col <= row).astype(BF16)
    tri_u = (col >= row).astype(BF16)
    parts = _split3(lf)
    g_l = _dot(tri_l, parts[0]) + _dot(tri_l, parts[1]) + _dot(tri_l, parts[2])
    g_u = _dot(tri_u, parts[0]) + _dot(tri_u, parts[1]) + _dot(tri_u, parts[2])
    lane = lax.broadcasted_iota(jnp.int32, (L, GATE_LANES), 1)
    g_sel = jnp.where(lane < nd + heads, g_l, g_u)
    g = pltpu.roll(g_sel, GATE_LANES - nd, 1)
    bc = pre - g
    gcol_ref[0] = g
    bcol_ref[0] = bc
    brow_ref[0] = jnp.transpose(bc)[0:8, :]


def _gate_fin(gates, bias, *, heads):
    b, _, n, _ = gates.shape
    L = SCAN_CHUNK
    assert 2 * heads == 8
    blk = pl.BlockSpec((1, L, GATE_LANES), lambda bi, c: (bi, c, 0))
    return pl.pallas_call(
        functools.partial(_gate_fin_kernel, L=L, heads=heads),
        grid=(b, n // L),
        in_specs=[pl.BlockSpec((1, heads, L, GATE_LANES), lambda bi, c: (bi, 0, c, 0)),
                  pl.BlockSpec((1, GATE_LANES), lambda bi, c: (0, 0))],
        out_specs=[blk, blk, pl.BlockSpec((1, 8, L), lambda bi, c: (bi, 0, c))],
        out_shape=[jax.ShapeDtypeStruct((b, n, GATE_LANES), F32),
                   jax.ShapeDtypeStruct((b, n, GATE_LANES), F32),
                   jax.ShapeDtypeStruct((b, 8, n), F32)],
        compiler_params=_cparams("parallel", "parallel"),
        name="gate_fin",
    )(gates, bias)


def _chain_step(q_ref, k_ref, kt_ref, v_ref, gc_ref, bc_ref, br_ref, out_ref, ct_ref, n_ref, m_ref,
                *, slot, head, lane_idx, lower, L, dh):
    hs = slice(head * dh, (head + 1) * dh)
    q = q_ref[0, :, hs]
    kt = kt_ref[0, hs, :]
    v = v_ref[0, :, hs]
    gcol = gc_ref[0, :, lane_idx:lane_idx + 1]
    bcol = bc_ref[0, :, lane_idx:lane_idx + 1]
    brow = br_ref[0, lane_idx:lane_idx + 1, :]
    m_prev = m_ref[slot][:, 0:1]

    row = lax.broadcasted_iota(jnp.int32, (L, L), 0)
    col = lax.broadcasted_iota(jnp.int32, (L, L), 1)
    mask = (col <= row) if lower else (col >= row)
    logw = jnp.where(mask, gcol + brow, -jnp.inf)
    log_prev = gcol + m_prev
    m_new = jnp.maximum(log_prev, jnp.max(logw, axis=1, keepdims=True))
    w_intra = jnp.exp(logw - m_new)
    w_prev = jnp.exp(log_prev - m_new)

    if out_ref is not None:
        p = _dot(q, kt) * w_intra
        n_row = n_ref[slot]
        qn = jnp.sum(q.astype(F32) * n_row, axis=1, keepdims=True)
        den = jnp.sum(p, axis=1, keepdims=True) + w_prev * qn
        num = _dot(p.astype(BF16), v) + w_prev * _dot(q, ct_ref[slot].astype(BF16))
        inv = 1.0 / jnp.maximum(jnp.abs(den), jnp.exp(-m_new))
        out_ref[0, :, hs] = (num * inv).astype(out_ref.dtype)

    rowv = lax.broadcasted_iota(jnp.int32, (L, 1), 0)
    last = (rowv == (L - 1 if lower else 0))
    g_last = jnp.sum(jnp.where(last, gcol, 0.0), axis=0, keepdims=True)
    m_end = jnp.sum(jnp.where(last, m_new, 0.0), axis=0, keepdims=True)
    w_src = jnp.exp(bcol + g_last - m_end)
    decay = jnp.exp(g_last + m_prev - m_end)
    vw = (w_src * v.astype(F32)).astype(BF16)
    ct_ref[slot] = decay * ct_ref[slot] + _dot(kt, vw)
    n_ref[slot] = decay * n_ref[slot] + jnp.sum(w_src * k_ref[0, :, hs].astype(F32), axis=0, keepdims=True)
    m_ref[slot] = jnp.broadcast_to(m_end, m_ref.shape[1:])


def _scan_kernel(*refs, L, heads, dh, emit_ctx):
    ctx_in = refs[0:7]
    fwd_in = refs[7:14]
    rev_in = refs[14:21]
    hf_ref, hr_ref = refs[21:23]
    if emit_ctx:
        hcf_ref, hcr_ref = refs[23:25]
        ct_ref, n_ref, m_ref = refs[25:28]
    else:
        hcf_ref = hcr_ref = None
        ct_ref, n_ref, m_ref = refs[23:26]
    s = pl.program_id(1)
    step = functools.partial(_chain_step, ct_ref=ct_ref, n_ref=n_ref, m_ref=m_ref, L=L, dh=dh)

    @pl.when(s == 0)
    def _():
        ct_ref[...] = jnp.zeros_like(ct_ref)
        n_ref[...] = jnp.zeros_like(n_ref)
        m_ref[...] = jnp.zeros_like(m_ref)
        for h in range(heads):
            step(*ctx_in, hcf_ref, slot=h, head=h, lane_idx=h, lower=True)
            step(*ctx_in, hcr_ref, slot=heads + h, head=h, lane_idx=heads + h, lower=False)

    @pl.when(s > 0)
    def _():
        for h in range(heads):
            step(*fwd_in, hf_ref, slot=h, head=h, lane_idx=h, lower=True)
            step(*rev_in, hr_ref, slot=heads + h, head=h, lane_idx=heads + h, lower=False)


def _scan(lat, ctx, *, heads, emit_ctx):
    b, n, e = lat["q"].shape
    nc = ctx["q"].shape[1]
    L = SCAN_CHUNK
    assert nc == L, "the context prefix is processed as exactly one chunk"
    dh = e // heads
    n_lat = n // L

    def specs(imap):
        tok = lambda bi, s: (bi, imap(s), 0)
        return [pl.BlockSpec((1, L, e), tok),
                pl.BlockSpec((1, L, e), tok),
                pl.BlockSpec((1, e, L), lambda bi, s: (bi, 0, imap(s))),
                pl.BlockSpec((1, L, e), tok),
                pl.BlockSpec((1, L, GATE_LANES), tok),
                pl.BlockSpec((1, L, GATE_LANES), tok),
                pl.BlockSpec((1, 8, L), lambda bi, s: (bi, 0, imap(s)))]

    def args(d):
        return [d["q"], d["k"], d["kt"], d["a"], d["gcol"], d["bcol"], d["brow"]]

    ctx_map = lambda s: 0
    fwd_map = lambda s: jnp.maximum(s - 1, 0)
    rev_map = lambda s: n_lat - jnp.maximum(s, 1)
    out_specs = [pl.BlockSpec((1, L, e), lambda bi, s: (bi, fwd_map(s), 0)),
                 pl.BlockSpec((1, L, e), lambda bi, s: (bi, rev_map(s), 0))]
    out_shape = [jax.ShapeDtypeStruct((b, n, e), BF16), jax.ShapeDtypeStruct((b, n, e), BF16)]
    if emit_ctx:
        out_specs += [pl.BlockSpec((1, L, e), lambda bi, s: (bi, 0, 0))] * 2
        out_shape += [jax.ShapeDtypeStruct((b, nc, e), BF16)] * 2
    return pl.pallas_call(
        functools.partial(_scan_kernel, L=L, heads=heads, dh=dh, emit_ctx=emit_ctx),
        grid=(b, n_lat + 1),
        in_specs=specs(ctx_map) + specs(fwd_map) + specs(rev_map),
        out_specs=out_specs,
        out_shape=out_shape,
        scratch_shapes=[pltpu.VMEM((2 * heads, dh, dh), F32),
                        pltpu.VMEM((2 * heads, 1, dh), F32),
                        pltpu.VMEM((2 * heads, 1, GATE_LANES), F32)],
        compiler_params=_cparams("parallel", "arbitrary"),
        name="mlstm_scan",
    )(*(args(ctx) + args(lat) + args(lat)))


def _mlstm_out_kernel(hf_ref, hr_ref, o_ref, z_ref, xc_ref, x_ref, lnw_ref, skip_ref, w_ref, mod_ref, out_ref,
                      *, d, dh, heads):
    hs = hf_ref[0].astype(F32) + hr_ref[0].astype(F32)
    parts = []
    for hh in range(heads):
        seg = hs[:, hh * dh:(hh + 1) * dh]
        cen = seg - jnp.mean(seg, axis=-1, keepdims=True)
        var = jnp.mean(cen * cen, axis=-1, keepdims=True)
        parts.append(cen * lax.rsqrt(var + EPS))
    hn = jnp.concatenate(parts, axis=1) * lnw_ref[...]
    hn = o_ref[0].astype(F32) * hn
    y = (hn + skip_ref[...] * xc_ref[0].astype(F32)) * z_ref[0].astype(F32)
    yo = _dot(y.astype(BF16), w_ref[...])
    out_ref[0] = x_ref[0] + mod_ref[0][:, 2 * d:3 * d] * yo


def _mlstm_out(hf, hr, a, xconv, x, lnw, skip, w, mod, *, tm, heads):
    b, n, d = x.shape
    e = hf.shape[2]
    per_batch = mod.shape[0] != 1
    mod_map = (lambda bi, t: (bi, 0, 0)) if per_batch else (lambda bi, t: (0, 0, 0))
    tok = lambda bi, t: (bi, t, 0)
    vec = pl.BlockSpec((1, e), lambda bi, t: (0, 0))
    return pl.pallas_call(
        functools.partial(_mlstm_out_kernel, d=d, dh=e // heads, heads=heads),
        grid=(b, n // tm),
        in_specs=[pl.BlockSpec((1, tm, e), tok),
                  pl.BlockSpec((1, tm, e), tok),
                  pl.BlockSpec((1, tm, e), lambda bi, t: (bi, t, 1)),
                  pl.BlockSpec((1, tm, e), lambda bi, t: (bi, t, 2)),
                  pl.BlockSpec((1, tm, e), tok),
                  pl.BlockSpec((1, tm, d), tok),
                  vec, vec,
                  pl.BlockSpec((e, d), lambda bi, t: (0, 0)),
                  pl.BlockSpec((1, 1, 3 * d), mod_map)],
        out_specs=pl.BlockSpec((1, tm, d), tok),
        out_shape=jax.ShapeDtypeStruct((b, n, d), F32),
        compiler_params=_cparams("parallel", "parallel"),
        name="mlstm_out",
    )(hf, hr, a, a, xconv, x, lnw, skip, w, mod)


def _chan_dft_kernel(u_ref, w_ref, a_ref, b_ref, *, groups, gd):
    for g in range(groups):
        r = _dot(u_ref[0, :, g * gd:(g + 1) * gd], w_ref[...])
        a_ref[0, :, g * gd:(g + 1) * gd] = r[:, :gd].astype(BF16)
        b_ref[0, :, g * gd:(g + 1) * gd] = r[:, gd:].astype(BF16)


def _chan_dft(uz, wc, *, tm, groups):
    b, n, e2 = uz.shape
    e = e2 // 2
    gd = e // groups
    tok = lambda bi, t: (bi, t, 0)
    return pl.pallas_call(
        functools.partial(_chan_dft_kernel, groups=groups, gd=gd),
        grid=(b, n // tm),
        in_specs=[pl.BlockSpec((1, tm, e), tok), pl.BlockSpec((gd, 2 * gd), lambda bi, t: (0, 0))],
        out_specs=[pl.BlockSpec((1, tm, e), tok)] * 2,
        out_shape=[jax.ShapeDtypeStruct((b, n, e), BF16)] * 2,
        compiler_params=_cparams("parallel", "parallel"),
        name="chan_dft",
    )(uz, wc)


def _pos_dft_kernel(cn_ref, sn_ref, a_ref, b_ref, y_ref):
    y_ref[0] = (_dot(cn_ref[...], a_ref[0]) - _dot(sn_ref[...], b_ref[0])).astype(y_ref.dtype)


def _pos_dft(cn, sn, a, bm, *, tm, tn):
    b, n, e = a.shape
    return pl.pallas_call(
        _pos_dft_kernel,
        grid=(b, n // tm, e // tn),
        in_specs=[pl.BlockSpec((tm, n), lambda bi, i, j: (i, 0)),
                  pl.BlockSpec((tm, n), lambda bi, i, j: (i, 0)),
                  pl.BlockSpec((1, n, tn), lambda bi, i, j: (bi, 0, j)),
                  pl.BlockSpec((1, n, tn), lambda bi, i, j: (bi, 0, j))],
        out_specs=pl.BlockSpec((1, tm, tn), lambda bi, i, j: (bi, i, j)),
        out_shape=jax.ShapeDtypeStruct((b, n, e), BF16),
        compiler_params=_cparams("parallel", "parallel", "parallel"),
        name="pos_dft",
    )(cn, sn, a, bm)


def _small_dft(zs):
    n = len(zs)
    if n == 1:
        return zs
    even = _small_dft(zs[0::2])
    odd = _small_dft(zs[1::2])
    out = [None] * n
    for k in range(n // 2):
        o_re, o_im = odd[k]
        if k == 0:
            t_re, t_im = o_re, o_im
        elif 4 * k == n:
            t_re, t_im = o_im, -o_re
        else:
            c = float(np.cos(2.0 * np.pi * k / n))
            s = float(np.sin(2.0 * np.pi * k / n))
            t_re, t_im = c * o_re + s * o_im, c * o_im - s * o_re
        e_re, e_im = even[k]
        out[k] = (e_re + t_re, e_im + t_im)
        out[k + n // 2] = (e_re - t_re, e_im - t_im)
    return out


def _fft_a_kernel(u_ref, wc_ref, tc_ref, ts_ref, gr_ref, gi_ref, *, n1, gd):
    zs = []
    for a in range(n1):
        r = _dot(u_ref[0, a], wc_ref[...])
        zs.append((r[:, :gd], r[:, gd:]))
    gs = _small_dft(zs)
    for k1 in range(n1):
        re, im = gs[k1]
        if k1 > 0:
            c = tc_ref[k1]
            s = ts_ref[k1]
            re, im = re * c + im * s, im * c - re * s
        gr_ref[0, k1] = re.astype(BF16)
        gi_ref[0, k1] = im.astype(BF16)


def _fft_a(uz, wc, tc, ts, *, n1, groups, tr):
    b, n, e2 = uz.shape
    e = e2 // 2
    gd = e // groups
    n2 = n // n1
    blk = pl.BlockSpec((1, n1, tr, gd), lambda bi, r, g: (bi, 0, r, g))
    tw = pl.BlockSpec((n1, tr, gd), lambda bi, r, g: (0, r, 0))
    return pl.pallas_call(
        functools.partial(_fft_a_kernel, n1=n1, gd=gd),
        grid=(b, n2 // tr, groups),
        in_specs=[blk, pl.BlockSpec((gd, 2 * gd), lambda bi, r, g: (0, 0)), tw, tw],
        out_specs=[blk, blk],
        out_shape=[jax.ShapeDtypeStruct((b, n1, n2, e), BF16)] * 2,
        compiler_params=_cparams("parallel", "parallel", "parallel"),
        name="fft_a",
    )(uz.reshape(b, n1, n2, e2), wc, tc, ts)


def _fft_b_kernel(wc_ref, ws_ref, gr_ref, gi_ref, y_ref, scr_ref, *, n1, n2):
    for k1 in range(n1):
        yk = _dot(wc_ref[...], gr_ref[0, k1]) + _dot(ws_ref[...], gi_ref[0, k1])
        for c in range(scr_ref.shape[0]):
            scr_ref[c, pl.ds(k1, n2, stride=n1), :] = yk[:, c * 128:(c + 1) * 128]
    for c in range(scr_ref.shape[0]):
        y_ref[0, :, c * 128:(c + 1) * 128] = scr_ref[c].astype(y_ref.dtype)


def _fft_b(w2c, w2s, gr, gi, *, tn):
    b, n1, n2, e = gr.shape
    blk = pl.BlockSpec((1, n1, n2, tn), lambda bi, j: (bi, 0, 0, j))
    wsp = pl.BlockSpec((n2, n2), lambda bi, j: (0, 0))
    return pl.pallas_call(
        functools.partial(_fft_b_kernel, n1=n1, n2=n2),
        grid=(b, e // tn),
        in_specs=[wsp, wsp, blk, blk],
        out_specs=pl.BlockSpec((1, n1 * n2, tn), lambda bi, j: (bi, 0, j)),
        out_shape=jax.ShapeDtypeStruct((b, n1 * n2, e), BF16),
        scratch_shapes=[pltpu.VMEM((tn // 128, n1 * n2, 128), F32)],
        compiler_params=_cparams("parallel", "parallel"),
        name="fft_b",
    )(w2c, w2s, gr, gi)


def _fnet_out_kernel(y_ref, z_ref, x_ref, w_ref, mod_ref, nf_ref, out_ref, *, d, final):
    y = y_ref[0].astype(F32) * z_ref[0].astype(F32)
    yo = _dot(y.astype(BF16), w_ref[...])
    xn = x_ref[0] + mod_ref[0][:, 2 * d:3 * d] * yo
    if final:
        xn = xn * lax.rsqrt(jnp.mean(xn * xn, axis=-1, keepdims=True) + EPS) * nf_ref[...]
    out_ref[0] = xn


def _fnet_out(y, uz, x, w, mod, nf, *, tm, final):
    b, n, d = x.shape
    e = y.shape[2]
    per_batch = mod.shape[0] != 1
    mod_map = (lambda bi, t: (bi, 0, 0)) if per_batch else (lambda bi, t: (0, 0, 0))
    tok = lambda bi, t: (bi, t, 0)
    return pl.pallas_call(
        functools.partial(_fnet_out_kernel, d=d, final=final),
        grid=(b, n // tm),
        in_specs=[pl.BlockSpec((1, tm, e), tok),
                  pl.BlockSpec((1, tm, e), lambda bi, t: (bi, t, 1)),
                  pl.BlockSpec((1, tm, d), tok),
                  pl.BlockSpec((e, d), lambda bi, t: (0, 0)),
                  pl.BlockSpec((1, 1, 3 * d), mod_map),
                  pl.BlockSpec((1, d), lambda bi, t: (0, 0))],
        out_specs=pl.BlockSpec((1, tm, d), tok),
        out_shape=jax.ShapeDtypeStruct((b, n, d), F32),
        compiler_params=_cparams("parallel", "parallel"),
        name="fnet_out",
    )(y, uz, x, w, mod, nf)


def _dft_mats(n, norm):
    k = lax.broadcasted_iota(jnp.int32, (n, n), 0)
    m = lax.broadcasted_iota(jnp.int32, (n, n), 1)
    ang = ((k * m) % n).astype(F32) * (2.0 * np.pi / n)
    return (jnp.cos(ang) * norm).astype(BF16), (jnp.sin(ang) * norm).astype(BF16)


def _mlstm_layer(xl, xc, mod_l, mod_c, gain, p, *, update_ctx):
    heads = MLSTM_HEADS
    n = xl.shape[1]
    nc = xc.shape[1]
    w_in = p["w_in"].astype(BF16)
    e = w_in.shape[1] // 3
    dh = e // heads
    cw = p["conv_w"].reshape(9, e)
    cb = p["conv_b"].reshape(1, e)
    wq = p["w_q"].astype(BF16)
    wk = p["w_k"].astype(BF16)
    wkt = jnp.swapaxes(p["w_k"], 1, 2).astype(BF16)
    wif = p["w_if"].reshape(2, 3, heads, dh, 2, heads)
    wg = jnp.transpose(wif, (2, 1, 3, 4, 0, 5)).reshape(heads, 3, dh, 4 * heads)
    wg = jnp.pad(wg, ((0, 0), (0, 0), (0, 0), (0, GATE_LANES - 4 * heads))).astype(BF16)
    bias = jnp.concatenate([p["b_i"].reshape(-1), p["b_f"].reshape(-1)])
    bias = jnp.pad(bias, (0, GATE_LANES - 4 * heads)).reshape(1, GATE_LANES)
    lnw = p["ln_w"].reshape(1, e)
    skip = p["skip"].reshape(1, e)
    w_out = p["w_out"].astype(BF16)

    def branch(x, mod, tm, img_w, has_vert):
        a = _norm_matmul(x, mod, gain, w_in, tm=tm, tn=e, acts=("none", "sigmoid", "silu"))
        xconv, q, k, kt, gates = _conv_qk(a, cw, cb, wq, wk, wkt, wg, tm=tm, img_w=img_w, has_vert=has_vert)
        gcol, bcol, brow = _gate_fin(gates, bias, heads=heads)
        return dict(a=a, xconv=xconv, q=q, k=k, kt=kt, gcol=gcol, bcol=bcol, brow=brow)

    lat = branch(xl, mod_l, min(1024, n), GRID_W, True)
    ctx = branch(xc, mod_c, nc, nc, False)
    outs = _scan(lat, ctx, heads=heads, emit_ctx=update_ctx)
    xl = _mlstm_out(outs[0], outs[1], lat["a"], lat["xconv"], xl, lnw, skip, w_out, mod_l,
                    tm=min(256, n), heads=heads)
    if update_ctx:
        xc = _mlstm_out(outs[2], outs[3], ctx["a"], ctx["xconv"], xc, lnw, skip, w_out, mod_c,
                        tm=nc, heads=heads)
    return xl, xc


def _fourier_layer(xl, xc, mod_l, mod_c, gain, p, nf, *, update_ctx, final):
    groups = FOURIER_GROUPS
    n = xl.shape[1]
    nc = xc.shape[1]
    w_in = p["w_in"].astype(BF16)
    w_out = p["w_out"].astype(BF16)
    e = w_in.shape[1] // 2
    gd = e // groups
    cc, sc = _dft_mats(gd, gd ** -0.5)
    wc = jnp.concatenate([cc, sc], axis=1)

    def position_dft(uz, m):
        n1 = FFT_SLABS if m % (FFT_SLABS * 256) == 0 else 1
        if n1 == 1:
            a, bm = _chan_dft(uz, wc, tm=min(1024, m), groups=groups)
            cn, sn = _dft_mats(m, m ** -0.5)
            return _pos_dft(cn, sn, a, bm, tm=min(512, m), tn=min(512, e))
        n2 = m // n1
        k1 = lax.broadcasted_iota(jnp.int32, (n1, n2, gd), 0)
        p2 = lax.broadcasted_iota(jnp.int32, (n1, n2, gd), 1)
        ang = ((k1 * p2) % m).astype(F32) * (2.0 * np.pi / m)
        gr, gi = _fft_a(uz, jnp.concatenate([cc, -sc], axis=1), jnp.cos(ang), jnp.sin(ang),
                        n1=n1, groups=groups, tr=min(512, n2))
        w2c, w2s = _dft_mats(n2, m ** -0.5)
        return _fft_b(w2c, w2s, gr, gi, tn=256)

    def branch(x, mod, tm, fin):
        m = x.shape[1]
        uz = _norm_matmul(x, mod, gain, w_in, tm=tm, tn=e, acts=("none", "silu"))
        y = position_dft(uz, m)
        return _fnet_out(y, uz, x, w_out, mod, nf, tm=min(512, m), final=fin)

    xl = branch(xl, mod_l, min(1024, n), final)
    if update_ctx:
        xc = branch(xc, mod_c, nc, False)
    return xl, xc


def kernel(x, c, ctx, c_ctx, norm_g, w_ada, b_ada, m_w_in, m_conv_w, m_conv_b, m_w_q, m_w_k, m_w_if, m_b_i, m_b_f,
           m_ln_w, m_skip, m_w_out, f_w_in, f_w_out, norm_f):
    b, n, d = x.shape
    depth = norm_g.shape[0]
    assert depth % N_MIXERS == 0, "the final norm is fused into the last Fourier layer"
    last_mlstm = ((depth - 1) // N_MIXERS) * N_MIXERS
    rows = -(-(b + 1) // 8) * 8
    cc = jnp.concatenate([c, c_ctx[None, :], jnp.zeros((rows - b - 1, d), F32)], axis=0)
    mods = _ada(cc, w_ada, b_ada)
    nf = norm_f.reshape(1, d)
    xl, xc = x, ctx
    for i in range(depth):
        j = i // N_MIXERS
        update_ctx = i < last_mlstm
        mod_l = mods[i, :b][:, None, :]
        mod_c = mods[i, b:b + 1][:, None, :]
        gain = norm_g[i].reshape(1, d)
        if i % N_MIXERS == 0:
            p = dict(w_in=m_w_in[j], conv_w=m_conv_w[j], conv_b=m_conv_b[j], w_q=m_w_q[j], w_k=m_w_k[j],
                     w_if=m_w_if[j], b_i=m_b_i[j], b_f=m_b_f[j], ln_w=m_ln_w[j], skip=m_skip[j], w_out=m_w_out[j])
            xl, xc = _mlstm_layer(xl, xc, mod_l, mod_c, gain, p, update_ctx=update_ctx)
        else:
            p = dict(w_in=f_w_in[j], w_out=f_w_out[j])
            xl, xc = _fourier_layer(xl, xc, mod_l, mod_c, gain, p, nf, update_ctx=update_ctx,
                                    final=(i == depth - 1))
    return xl
```

```python
import functools

import numpy as np
import jax
import jax.numpy as jnp
from jax import lax
from jax.experimental import pallas as pl
from jax.experimental.pallas import tpu as pltpu

F32 = jnp.float32
BF16 = jnp.bfloat16

GRID_W = 64
N_MIXERS = 2
MLSTM_HEADS = 4
FOURIER_GROUPS = 8
EPS = 1e-6
SCAN_CHUNK = 256
CONV_SUB = 256
GATE_LANES = 128
FFT_SLABS = 8
VMEM_LIMIT = 56 * 1024 * 1024


def _cparams(*sem):
    return pltpu.CompilerParams(dimension_semantics=sem, vmem_limit_bytes=VMEM_LIMIT)


def _sigmoid(v):
    return 1.0 / (1.0 + jnp.exp(-v))


def _silu(v):
    return v * _sigmoid(v)


def _dot(a, b):
    return jnp.dot(a, b, preferred_element_type=F32)


def _dot_nt(a, b):
    return lax.dot_general(a, b, (((1,), (1,)), ((), ())), preferred_element_type=F32)


def _ada_kernel(c_ref, w_ref, b_ref, o_ref):
    s = _silu(c_ref[...])
    o_ref[0] = _dot(s.astype(BF16), w_ref[0].astype(BF16)) + b_ref[0]


def _ada(cc, w_ada, b_ada):
    depth, d, d3 = w_ada.shape
    r = cc.shape[0]
    tn = 512
    return pl.pallas_call(
        _ada_kernel,
        grid=(depth, d3 // tn),
        in_specs=[pl.BlockSpec((r, d), lambda i, j: (0, 0)),
                  pl.BlockSpec((1, d, tn), lambda i, j: (i, 0, j)),
                  pl.BlockSpec((1, 1, tn), lambda i, j: (i, 0, j))],
        out_specs=pl.BlockSpec((1, r, tn), lambda i, j: (i, 0, j)),
        out_shape=jax.ShapeDtypeStruct((depth, r, d3), F32),
        compiler_params=_cparams("parallel", "parallel"),
        name="ada",
    )(cc, w_ada, b_ada.reshape(depth, 1, d3))


def _norm_matmul_kernel(x_ref, mod_ref, g_ref, w_ref, o_ref, *, d):
    x = x_ref[0]
    y = x * lax.rsqrt(jnp.mean(x * x, axis=-1, keepdims=True) + EPS) * g_ref[...]
    mod = mod_ref[0]
    h = y * (1.0 + mod[:, d:2 * d]) + mod[:, :d]
    o_ref[0] = _dot(h.astype(BF16), w_ref[...]).astype(o_ref.dtype)


def _norm_matmul(x, mod, gain, w, *, tm, tn):
    b, n, d = x.shape
    no = w.shape[1]
    per_batch = mod.shape[0] != 1
    mod_map = (lambda j, bi, t: (bi, 0, 0)) if per_batch else (lambda j, bi, t: (0, 0, 0))
    return pl.pallas_call(
        functools.partial(_norm_matmul_kernel, d=d),
        grid=(no // tn, b, n // tm),
        in_specs=[pl.BlockSpec((1, tm, d), lambda j, bi, t: (bi, t, 0)),
                  pl.BlockSpec((1, 1, 3 * d), mod_map),
                  pl.BlockSpec((1, d), lambda j, bi, t: (0, 0)),
                  pl.BlockSpec((d, tn), lambda j, bi, t: (0, j))],
        out_specs=pl.BlockSpec((1, tm, tn), lambda j, bi, t: (bi, t, j)),
        out_shape=jax.ShapeDtypeStruct((b, n, no), BF16),
        compiler_params=_cparams("parallel", "parallel", "parallel"),
        name="norm_matmul",
    )(x, mod, gain, w)


def _conv_qk_kernel(xc_ref, xp_ref, xn_ref, cw_ref, cb_ref, wq_ref, wk_ref, wkt_ref, wg_ref,
                    xconv_ref, q_ref, k_ref, kt_ref, gates_ref, *, tm, img_w, has_vert, n_tiles, scale):
    t = pl.program_id(1)
    sub = min(tm, CONV_SUB)
    n_sub = tm // sub
    for c in range(n_sub):
        lo = c * sub
        xcb = xc_ref[0, lo:lo + sub, :]
        xc = xcb.astype(F32)
        if has_vert:
            if c == 0:
                top = jnp.where(t > 0, xp_ref[0].astype(F32), 0.0)
            else:
                top = xc_ref[0, lo - img_w:lo, :].astype(F32)
            if c == n_sub - 1:
                bot = jnp.where(t < n_tiles - 1, xn_ref[0].astype(F32), 0.0)
            else:
                bot = xc_ref[0, lo + sub:lo + sub + img_w, :].astype(F32)
            ext = jnp.concatenate([top, xc, bot], axis=0)
            off = img_w
            taps = (-1, 0, 1)
        else:
            ext = xc
            off = 0
            taps = (0,)
        rows = ext.shape[0]
        col = lax.broadcasted_iota(jnp.int32, (rows, 1), 0) % img_w
        left = jnp.where(col == 0, 0.0, pltpu.roll(ext, 1, 0))
        right = jnp.where(col == img_w - 1, 0.0, pltpu.roll(ext, rows - 1, 0))
        acc = None
        for dr in taps:
            s = off + dr * img_w
            r = 3 * (dr + 1)
            term = (cw_ref[r:r + 1, :] * left[s:s + sub] + cw_ref[r + 1:r + 2, :] * ext[s:s + sub]
                    + cw_ref[r + 2:r + 3, :] * right[s:s + sub])
            acc = term if acc is None else acc + term
        xv = _silu(acc + cb_ref[...]).astype(BF16)
        xconv_ref[0, lo:lo + sub, :] = xv
        qb = _dot(xv, wq_ref[0]).astype(BF16)
        kf = _dot(xv, wk_ref[0]) * scale
        kb = kf.astype(BF16)
        q_ref[0, lo:lo + sub, :] = qb
        k_ref[0, lo:lo + sub, :] = kb
        kt_ref[0, :, lo:lo + sub] = jnp.transpose(kf).astype(BF16)
        gates_ref[0, 0, lo:lo + sub, :] = (_dot(qb, wg_ref[0, 0]) + _dot(kb, wg_ref[0, 1])
                                           + _dot(xcb, wg_ref[0, 2]))


def _conv_qk(a, cw, cb, wq, wk, wkt, wg, *, tm, img_w, has_vert):
    b, n, _ = a.shape
    heads, dh, _ = wq.shape
    e = heads * dh
    n_tiles = n // tm
    hb = img_w if has_vert else 16
    per = tm // hb
    last = n // hb - 1
    kern = functools.partial(_conv_qk_kernel, tm=tm, img_w=img_w, has_vert=has_vert, n_tiles=n_tiles,
                             scale=float(dh) ** -0.5)
    tok = lambda bi, t, h: (bi, t, h)
    return pl.pallas_call(
        kern,
        grid=(b, n_tiles, heads),
        in_specs=[pl.BlockSpec((1, tm, dh), tok),
                  pl.BlockSpec((1, hb, dh), lambda bi, t, h: (bi, jnp.maximum(t * per - 1, 0), h)),
                  pl.BlockSpec((1, hb, dh), lambda bi, t, h: (bi, jnp.minimum((t + 1) * per, last), h)),
                  pl.BlockSpec((9, dh), lambda bi, t, h: (0, h)),
                  pl.BlockSpec((1, dh), lambda bi, t, h: (0, h)),
                  pl.BlockSpec((1, dh, dh), lambda bi, t, h: (h, 0, 0)),
                  pl.BlockSpec((1, dh, dh), lambda bi, t, h: (h, 0, 0)),
                  pl.BlockSpec((1, dh, dh), lambda bi, t, h: (h, 0, 0)),
                  pl.BlockSpec((1, 3, dh, GATE_LANES), lambda bi, t, h: (h, 0, 0, 0))],
        out_specs=[pl.BlockSpec((1, tm, dh), tok),
                   pl.BlockSpec((1, tm, dh), tok),
                   pl.BlockSpec((1, tm, dh), tok),
                   pl.BlockSpec((1, dh, tm), lambda bi, t, h: (bi, h, t)),
                   pl.BlockSpec((1, 1, tm, GATE_LANES), lambda bi, t, h: (bi, h, t, 0))],
        out_shape=[jax.ShapeDtypeStruct((b, n, e), BF16),
                   jax.ShapeDtypeStruct((b, n, e), BF16),
                   jax.ShapeDtypeStruct((b, n, e), BF16),
                   jax.ShapeDtypeStruct((b, e, n), BF16),
                   jax.ShapeDtypeStruct((b, heads, n, GATE_LANES), F32)],
        compiler_params=_cparams("parallel", "parallel", "parallel"),
        name="conv_qk",
    )(a, a, a, cw, cb, wq, wk, wkt, wg)


def _split3(a):
    hi = a.astype(BF16)
    r1 = a - hi.astype(F32)
    mid = r1.astype(BF16)
    lo = (r1 - mid.astype(F32)).astype(BF16)
    return hi, mid, lo


def _gate_fin_kernel(g_ref, bias_ref, gcol_ref, bcol_ref, brow_ref, *, L, heads):
    nd = 2 * heads
    pre = jnp.sum(g_ref[0], axis=0) + bias_ref[...]
    lf = jnp.minimum(pre, 0.0) - jnp.log(1.0 + jnp.exp(-jnp.abs(pre)))
    row = lax.broadcasted_iota(jnp.int32, (L, L), 0)
    col = lax.broadcasted_iota(jnp.int32, (L, L), 1)
    tri_l = (col <= row).astype(BF16)
    tri_u = (col >= row).astype(BF16)
    parts = _split3(lf)
    g_l = _dot(tri_l, parts[0]) + _dot(tri_l, parts[1]) + _dot(tri_l, parts[2])
    g_u = _dot(tri_u, parts[0]) + _dot(tri_u, parts[1]) + _dot(tri_u, parts[2])
    lane = lax.broadcasted_iota(jnp.int32, (L, GATE_LANES), 1)
    g_sel = jnp.where(lane < nd + heads, g_l, g_u)
    g = pltpu.roll(g_sel, GATE_LANES - nd, 1)
    bc = pre - g
    gcol_ref[0] = g
    bcol_ref[0] = bc
    brow_ref[0] = jnp.transpose(bc)[0:8, :]


def _gate_fin(gates, bias, *, heads):
    b, _, n, _ = gates.shape
    L = SCAN_CHUNK
    assert 2 * heads == 8
    blk = pl.BlockSpec((1, L, GATE_LANES), lambda bi, c: (bi, c, 0))
    return pl.pallas_call(
        functools.partial(_gate_fin_kernel, L=L, heads=heads),
        grid=(b, n // L),
        in_specs=[pl.BlockSpec((1, heads, L, GATE_LANES), lambda bi, c: (bi, 0, c, 0)),
                  pl.BlockSpec((1, GATE_LANES), lambda bi, c: (0, 0))],
        out_specs=[blk, blk, pl.BlockSpec((1, 8, L), lambda bi, c: (bi, 0, c))],
        out_shape=[jax.ShapeDtypeStruct((b, n, GATE_LANES), F32),
                   jax.ShapeDtypeStruct((b, n, GATE_LANES), F32),
                   jax.ShapeDtypeStruct((b, 8, n), F32)],
        compiler_params=_cparams("parallel", "parallel"),
        name="gate_fin",
    )(gates, bias)


def _chain_step(q_ref, k_ref, kt_ref, v_ref, gc_ref, bc_ref, br_ref, out_ref, ct_ref, n_ref, m_ref,
                *, slot, head, lane_idx, lower, L, dh):
    hs = slice(head * dh, (head + 1) * dh)
    q = q_ref[0, :, hs]
    kt = kt_ref[0, hs, :]
    v = v_ref[0, :, hs]
    gcol = gc_ref[0, :, lane_idx:lane_idx + 1]
    bcol = bc_ref[0, :, lane_idx:lane_idx + 1]
    brow = br_ref[0, lane_idx:lane_idx + 1, :]
    m_prev = m_ref[slot][:, 0:1]

    row = lax.broadcasted_iota(jnp.int32, (L, L), 0)
    col = lax.broadcasted_iota(jnp.int32, (L, L), 1)
    mask = (col <= row) if lower else (col >= row)
    logw = jnp.where(mask, gcol + brow, -jnp.inf)
    log_prev = gcol + m_prev
    m_new = jnp.maximum(log_prev, jnp.max(logw, axis=1, keepdims=True))
    w_intra = jnp.exp(logw - m_new)
    w_prev = jnp.exp(log_prev - m_new)

    if out_ref is not None:
        p = _dot(q, kt) * w_intra
        n_row = n_ref[slot]
        qn = jnp.sum(q.astype(F32) * n_row, axis=1, keepdims=True)
        den = jnp.sum(p, axis=1, keepdims=True) + w_prev * qn
        num = _dot(p.astype(BF16), v) + w_prev * _dot(q, ct_ref[slot].astype(BF16))
        inv = 1.0 / jnp.maximum(jnp.abs(den), jnp.exp(-m_new))
        out_ref[0, :, hs] = (num * inv).astype(out_ref.dtype)

    rowv = lax.broadcasted_iota(jnp.int32, (L, 1), 0)
    last = (rowv == (L - 1 if lower else 0))
    g_last = jnp.sum(jnp.where(last, gcol, 0.0), axis=0, keepdims=True)
    m_end = jnp.sum(jnp.where(last, m_new, 0.0), axis=0, keepdims=True)
    w_src = jnp.exp(bcol + g_last - m_end)
    decay = jnp.exp(g_last + m_prev - m_end)
    vw = (w_src * v.astype(F32)).astype(BF16)
    ct_ref[slot] = decay * ct_ref[slot] + _dot(kt, vw)
    n_ref[slot] = decay * n_ref[slot] + jnp.sum(w_src * k_ref[0, :, hs].astype(F32), axis=0, keepdims=True)
    m_ref[slot] = jnp.broadcast_to(m_end, m_ref.shape[1:])


def _scan_kernel(*refs, L, heads, dh, emit_ctx):
    ctx_in = refs[0:7]
    fwd_in = refs[7:14]
    rev_in = refs[14:21]
    hf_ref, hr_ref = refs[21:23]
    if emit_ctx:
        hcf_ref, hcr_ref = refs[23:25]
        ct_ref, n_ref, m_ref = refs[25:28]
    else:
        hcf_ref = hcr_ref = None
        ct_ref, n_ref, m_ref = refs[23:26]
    s = pl.program_id(1)
    step = functools.partial(_chain_step, ct_ref=ct_ref, n_ref=n_ref, m_ref=m_ref, L=L, dh=dh)

    @pl.when(s == 0)
    def _():
        ct_ref[...] = jnp.zeros_like(ct_ref)
        n_ref[...] = jnp.zeros_like(n_ref)
        m_ref[...] = jnp.zeros_like(m_ref)
        for h in range(heads):
            step(*ctx_in, hcf_ref, slot=h, head=h, lane_idx=h, lower=True)
            step(*ctx_in, hcr_ref, slot=heads + h, head=h, lane_idx=heads + h, lower=False)

    @pl.when(s > 0)
    def _():
        for h in range(heads):
            step(*fwd_in, hf_ref, slot=h, head=h, lane_idx=h, lower=True)
            step(*rev_in, hr_ref, slot=heads + h, head=h, lane_idx=heads + h, lower=False)


def _scan(lat, ctx, *, heads, emit_ctx):
    b, n, e = lat["q"].shape
    nc = ctx["q"].shape[1]
    L = SCAN_CHUNK
    assert nc == L, "the context prefix is processed as exactly one chunk"
    dh = e // heads
    n_lat = n // L

    def specs(imap):
        tok = lambda bi, s: (bi, imap(s), 0)
        return [pl.BlockSpec((1, L, e), tok),
                pl.BlockSpec((1, L, e), tok),
                pl.BlockSpec((1, e, L), lambda bi, s: (bi, 0, imap(s))),
                pl.BlockSpec((1, L, e), tok),
                pl.BlockSpec((1, L, GATE_LANES), tok),
                pl.BlockSpec((1, L, GATE_LANES), tok),
                pl.BlockSpec((1, 8, L), lambda bi, s: (bi, 0, imap(s)))]

    def args(d):
        return [d["q"], d["k"], d["kt"], d["a"], d["gcol"], d["bcol"], d["brow"]]

    ctx_map = lambda s: 0
    fwd_map = lambda s: jnp.maximum(s - 1, 0)
    rev_map = lambda s: n_lat - jnp.maximum(s, 1)
    out_specs = [pl.BlockSpec((1, L, e), lambda bi, s: (bi, fwd_map(s), 0)),
                 pl.BlockSpec((1, L, e), lambda bi, s: (bi, rev_map(s), 0))]
    out_shape = [jax.ShapeDtypeStruct((b, n, e), BF16), jax.ShapeDtypeStruct((b, n, e), BF16)]
    if emit_ctx:
        out_specs += [pl.BlockSpec((1, L, e), lambda bi, s: (bi, 0, 0))] * 2
        out_shape += [jax.ShapeDtypeStruct((b, nc, e), BF16)] * 2
    return pl.pallas_call(
        functools.partial(_scan_kernel, L=L, heads=heads, dh=dh, emit_ctx=emit_ctx),
        grid=(b, n_lat + 1),
        in_specs=specs(ctx_map) + specs(fwd_map) + specs(rev_map),
        out_specs=out_specs,
        out_shape=out_shape,
        scratch_shapes=[pltpu.VMEM((2 * heads, dh, dh), F32),
                        pltpu.VMEM((2 * heads, 1, dh), F32),
                        pltpu.VMEM((2 * heads, 1, GATE_LANES), F32)],
        compiler_params=_cparams("parallel", "arbitrary"),
        name="mlstm_scan",
    )(*(args(ctx) + args(lat) + args(lat)))


def _mlstm_out_kernel(hf_ref, hr_ref, o_ref, z_ref, xc_ref, x_ref, lnw_ref, skip_ref, w_ref, mod_ref, out_ref,
                      *, d, dh, heads):
    hs = hf_ref[0].astype(F32) + hr_ref[0].astype(F32)
    parts = []
    for hh in range(heads):
        seg = hs[:, hh * dh:(hh + 1) * dh]
        cen = seg - jnp.mean(seg, axis=-1, keepdims=True)
        var = jnp.mean(cen * cen, axis=-1, keepdims=True)
        parts.append(cen * lax.rsqrt(var + EPS))
    hn = jnp.concatenate(parts, axis=1) * lnw_ref[...]
    hn = _sigmoid(o_ref[0].astype(F32)) * hn
    y = (hn + skip_ref[...] * xc_ref[0].astype(F32)) * _silu(z_ref[0].astype(F32))
    yo = _dot(y.astype(BF16), w_ref[...])
    out_ref[0] = x_ref[0] + mod_ref[0][:, 2 * d:3 * d] * yo


def _mlstm_out(hf, hr, a, xconv, x, lnw, skip, w, mod, *, tm, heads):
    b, n, d = x.shape
    e = hf.shape[2]
    per_batch = mod.shape[0] != 1
    mod_map = (lambda bi, t: (bi, 0, 0)) if per_batch else (lambda bi, t: (0, 0, 0))
    tok = lambda bi, t: (bi, t, 0)
    vec = pl.BlockSpec((1, e), lambda bi, t: (0, 0))
    return pl.pallas_call(
        functools.partial(_mlstm_out_kernel, d=d, dh=e // heads, heads=heads),
        grid=(b, n // tm),
        in_specs=[pl.BlockSpec((1, tm, e), tok),
                  pl.BlockSpec((1, tm, e), tok),
                  pl.BlockSpec((1, tm, e), lambda bi, t: (bi, t, 1)),
                  pl.BlockSpec((1, tm, e), lambda bi, t: (bi, t, 2)),
                  pl.BlockSpec((1, tm, e), tok),
                  pl.BlockSpec((1, tm, d), tok),
                  vec, vec,
                  pl.BlockSpec((e, d), lambda bi, t: (0, 0)),
                  pl.BlockSpec((1, 1, 3 * d), mod_map)],
        out_specs=pl.BlockSpec((1, tm, d), tok),
        out_shape=jax.ShapeDtypeStruct((b, n, d), F32),
        compiler_params=_cparams("parallel", "parallel"),
        name="mlstm_out",
    )(hf, hr, a, a, xconv, x, lnw, skip, w, mod)


def _chan_dft_kernel(u_ref, w_ref, a_ref, b_ref, *, groups, gd):
    for g in range(groups):
        r = _dot(u_ref[0, :, g * gd:(g + 1) * gd], w_ref[...])
        a_ref[0, :, g * gd:(g + 1) * gd] = r[:, :gd].astype(BF16)
        b_ref[0, :, g * gd:(g + 1) * gd] = r[:, gd:].astype(BF16)


def _chan_dft(uz, wc, *, tm, groups):
    b, n, e2 = uz.shape
    e = e2 // 2
    gd = e // groups
    tok = lambda bi, t: (bi, t, 0)
    return pl.pallas_call(
        functools.partial(_chan_dft_kernel, groups=groups, gd=gd),
        grid=(b, n // tm),
        in_specs=[pl.BlockSpec((1, tm, e), tok), pl.BlockSpec((gd, 2 * gd), lambda bi, t: (0, 0))],
        out_specs=[pl.BlockSpec((1, tm, e), tok)] * 2,
        out_shape=[jax.ShapeDtypeStruct((b, n, e), BF16)] * 2,
        compiler_params=_cparams("parallel", "parallel"),
        name="chan_dft",
    )(uz, wc)


def _pos_dft_kernel(cn_ref, sn_ref, a_ref, b_ref, y_ref):
    y_ref[0] = (_dot(cn_ref[...], a_ref[0]) - _dot(sn_ref[...], b_ref[0])).astype(y_ref.dtype)


def _pos_dft(cn, sn, a, bm, *, tm, tn):
    b, n, e = a.shape
    return pl.pallas_call(
        _pos_dft_kernel,
        grid=(b, n // tm, e // tn),
        in_specs=[pl.BlockSpec((tm, n), lambda bi, i, j: (i, 0)),
                  pl.BlockSpec((tm, n), lambda bi, i, j: (i, 0)),
                  pl.BlockSpec((1, n, tn), lambda bi, i, j: (bi, 0, j)),
                  pl.BlockSpec((1, n, tn), lambda bi, i, j: (bi, 0, j))],
        out_specs=pl.BlockSpec((1, tm, tn), lambda bi, i, j: (bi, i, j)),
        out_shape=jax.ShapeDtypeStruct((b, n, e), BF16),
        compiler_params=_cparams("parallel", "parallel", "parallel"),
        name="pos_dft",
    )(cn, sn, a, bm)


def _small_dft(zs):
    n = len(zs)
    if n == 1:
        return zs
    even = _small_dft(zs[0::2])
    odd = _small_dft(zs[1::2])
    out = [None] * n
    for k in range(n // 2):
        o_re, o_im = odd[k]
        if k == 0:
            t_re, t_im = o_re, o_im
        elif 4 * k == n:
            t_re, t_im = o_im, -o_re
        else:
            c = float(np.cos(2.0 * np.pi * k / n))
            s = float(np.sin(2.0 * np.pi * k / n))
            t_re, t_im = c * o_re + s * o_im, c * o_im - s * o_re
        e_re, e_im = even[k]
        out[k] = (e_re + t_re, e_im + t_im)
        out[k + n // 2] = (e_re - t_re, e_im - t_im)
    return out


def _fft_a_kernel(u_ref, wc_ref, tc_ref, ts_ref, gr_ref, gi_ref, *, n1, gd):
    zs = []
    for a in range(n1):
        r = _dot(u_ref[0, a], wc_ref[...])
        zs.append((r[:, :gd], r[:, gd:]))
    gs = _small_dft(zs)
    for k1 in range(n1):
        re, im = gs[k1]
        if k1 > 0:
            c = tc_ref[k1]
            s = ts_ref[k1]
            re, im = re * c + im * s, im * c - re * s
        gr_ref[0, k1] = re.astype(BF16)
        gi_ref[0, k1] = im.astype(BF16)


def _fft_a(uz, wc, tc, ts, *, n1, groups, tr):
    b, n, e2 = uz.shape
    e = e2 // 2
    gd = e // groups
    n2 = n // n1
    blk = pl.BlockSpec((1, n1, tr, gd), lambda bi, r, g: (bi, 0, r, g))
    tw = pl.BlockSpec((n1, tr, gd), lambda bi, r, g: (0, r, 0))
    return pl.pallas_call(
        functools.partial(_fft_a_kernel, n1=n1, gd=gd),
        grid=(b, n2 // tr, groups),
        in_specs=[blk, pl.BlockSpec((gd, 2 * gd), lambda bi, r, g: (0, 0)), tw, tw],
        out_specs=[blk, blk],
        out_shape=[jax.ShapeDtypeStruct((b, n1, n2, e), BF16)] * 2,
        compiler_params=_cparams("parallel", "parallel", "parallel"),
        name="fft_a",
    )(uz.reshape(b, n1, n2, e2), wc, tc, ts)


def _fft_b_kernel(wc_ref, ws_ref, gr_ref, gi_ref, y_ref, scr_ref, *, n1, n2):
    for k1 in range(n1):
        yk = _dot(wc_ref[...], gr_ref[0, k1]) + _dot(ws_ref[...], gi_ref[0, k1])
        for c in range(scr_ref.shape[0]):
            scr_ref[c, pl.ds(k1, n2, stride=n1), :] = yk[:, c * 128:(c + 1) * 128]
    for c in range(scr_ref.shape[0]):
        y_ref[0, :, c * 128:(c + 1) * 128] = scr_ref[c].astype(y_ref.dtype)


def _fft_b(w2c, w2s, gr, gi, *, tn):
    b, n1, n2, e = gr.shape
    blk = pl.BlockSpec((1, n1, n2, tn), lambda bi, j: (bi, 0, 0, j))
    wsp = pl.BlockSpec((n2, n2), lambda bi, j: (0, 0))
    return pl.pallas_call(
        functools.partial(_fft_b_kernel, n1=n1, n2=n2),
        grid=(b, e // tn),
        in_specs=[wsp, wsp, blk, blk],
        out_specs=pl.BlockSpec((1, n1 * n2, tn), lambda bi, j: (bi, 0, j)),
        out_shape=jax.ShapeDtypeStruct((b, n1 * n2, e), BF16),
        scratch_shapes=[pltpu.VMEM((tn // 128, n1 * n2, 128), F32)],
        compiler_params=_cparams("parallel", "parallel"),
        name="fft_b",
    )(w2c, w2s, gr, gi)


def _fnet_out_kernel(y_ref, z_ref, x_ref, w_ref, mod_ref, nf_ref, out_ref, *, d, final):
    y = y_ref[0].astype(F32) * _silu(z_ref[0].astype(F32))
    yo = _dot(y.astype(BF16), w_ref[...])
    xn = x_ref[0] + mod_ref[0][:, 2 * d:3 * d] * yo
    if final:
        xn = xn * lax.rsqrt(jnp.mean(xn * xn, axis=-1, keepdims=True) + EPS) * nf_ref[...]
    out_ref[0] = xn


def _fnet_out(y, uz, x, w, mod, nf, *, tm, final):
    b, n, d = x.shape
    e = y.shape[2]
    per_batch = mod.shape[0] != 1
    mod_map = (lambda bi, t: (bi, 0, 0)) if per_batch else (lambda bi, t: (0, 0, 0))
    tok = lambda bi, t: (bi, t, 0)
    return pl.pallas_call(
        functools.partial(_fnet_out_kernel, d=d, final=final),
        grid=(b, n // tm),
        in_specs=[pl.BlockSpec((1, tm, e), tok),
                  pl.BlockSpec((1, tm, e), lambda bi, t: (bi, t, 1)),
                  pl.BlockSpec((1, tm, d), tok),
                  pl.BlockSpec((e, d), lambda bi, t: (0, 0)),
                  pl.BlockSpec((1, 1, 3 * d), mod_map),
                  pl.BlockSpec((1, d), lambda bi, t: (0, 0))],
        out_specs=pl.BlockSpec((1, tm, d), tok),
        out_shape=jax.ShapeDtypeStruct((b, n, d), F32),
        compiler_params=_cparams("parallel", "parallel"),
        name="fnet_out",
    )(y, uz, x, w, mod, nf)


def _dft_mats(n, norm):
    k = lax.broadcasted_iota(jnp.int32, (n, n), 0)
    m = lax.broadcasted_iota(jnp.int32, (n, n), 1)
    ang = ((k * m) % n).astype(F32) * (2.0 * np.pi / n)
    return (jnp.cos(ang) * norm).astype(BF16), (jnp.sin(ang) * norm).astype(BF16)


def _mlstm_layer(xl, xc, mod_l, mod_c, gain, p, *, update_ctx):
    heads = MLSTM_HEADS
    n = xl.shape[1]
    nc = xc.shape[1]
    w_in = p["w_in"].astype(BF16)
    e = w_in.shape[1] // 3
    dh = e // heads
    cw = p["conv_w"].reshape(9, e)
    cb = p["conv_b"].reshape(1, e)
    wq = p["w_q"].astype(BF16)
    wk = p["w_k"].astype(BF16)
    wkt = jnp.swapaxes(p["w_k"], 1, 2).astype(BF16)
    wif = p["w_if"].reshape(2, 3, heads, dh, 2, heads)
    wg = jnp.transpose(wif, (2, 1, 3, 4, 0, 5)).reshape(heads, 3, dh, 4 * heads)
    wg = jnp.pad(wg, ((0, 0), (0, 0), (0, 0), (0, GATE_LANES - 4 * heads))).astype(BF16)
    bias = jnp.concatenate([p["b_i"].reshape(-1), p["b_f"].reshape(-1)])
    bias = jnp.pad(bias, (0, GATE_LANES - 4 * heads)).reshape(1, GATE_LANES)
    lnw = p["ln_w"].reshape(1, e)
    skip = p["skip"].reshape(1, e)
    w_out = p["w_out"].astype(BF16)

    def branch(x, mod, tm, img_w, has_vert):
        a = _norm_matmul(x, mod, gain, w_in, tm=tm, tn=e)
        xconv, q, k, kt, gates = _conv_qk(a, cw, cb, wq, wk, wkt, wg, tm=tm, img_w=img_w, has_vert=has_vert)
        gcol, bcol, brow = _gate_fin(gates, bias, heads=heads)
        return dict(a=a, xconv=xconv, q=q, k=k, kt=kt, gcol=gcol, bcol=bcol, brow=brow)

    lat = branch(xl, mod_l, min(1024, n), GRID_W, True)
    ctx = branch(xc, mod_c, nc, nc, False)
    outs = _scan(lat, ctx, heads=heads, emit_ctx=update_ctx)
    xl = _mlstm_out(outs[0], outs[1], lat["a"], lat["xconv"], xl, lnw, skip, w_out, mod_l,
                    tm=min(256, n), heads=heads)
    if update_ctx:
        xc = _mlstm_out(outs[2], outs[3], ctx["a"], ctx["xconv"], xc, lnw, skip, w_out, mod_c,
                        tm=nc, heads=heads)
    return xl, xc


def _fourier_layer(xl, xc, mod_l, mod_c, gain, p, nf, *, update_ctx, final):
    groups = FOURIER_GROUPS
    n = xl.shape[1]
    nc = xc.shape[1]
    w_in = p["w_in"].astype(BF16)
    w_out = p["w_out"].astype(BF16)
    e = w_in.shape[1] // 2
    gd = e // groups
    cc, sc = _dft_mats(gd, gd ** -0.5)
    wc = jnp.concatenate([cc, sc], axis=1)

    def position_dft(uz, m):
        n1 = FFT_SLABS if m % (FFT_SLABS * 256) == 0 else 1
        if n1 == 1:
            a, bm = _chan_dft(uz, wc, tm=min(1024, m), groups=groups)
            cn, sn = _dft_mats(m, m ** -0.5)
            return _pos_dft(cn, sn, a, bm, tm=min(512, m), tn=min(512, e))
        n2 = m // n1
        k1 = lax.broadcasted_iota(jnp.int32, (n1, n2, gd), 0)
        p2 = lax.broadcasted_iota(jnp.int32, (n1, n2, gd), 1)
        ang = ((k1 * p2) % m).astype(F32) * (2.0 * np.pi / m)
        gr, gi = _fft_a(uz, jnp.concatenate([cc, -sc], axis=1), jnp.cos(ang), jnp.sin(ang),
                        n1=n1, groups=groups, tr=min(256, n2))
        w2c, w2s = _dft_mats(n2, m ** -0.5)
        return _fft_b(w2c, w2s, gr, gi, tn=256)

    def branch(x, mod, tm, fin):
        m = x.shape[1]
        uz = _norm_matmul(x, mod, gain, w_in, tm=tm, tn=e)
        y = position_dft(uz, m)
        return _fnet_out(y, uz, x, w_out, mod, nf, tm=min(512, m), final=fin)

    xl = branch(xl, mod_l, min(1024, n), final)
    if update_ctx:
        xc = branch(xc, mod_c, nc, False)
    return xl, xc


def kernel(x, c, ctx, c_ctx, norm_g, w_ada, b_ada, m_w_in, m_conv_w, m_conv_b, m_w_q, m_w_k, m_w_if, m_b_i, m_b_f,
           m_ln_w, m_skip, m_w_out, f_w_in, f_w_out, norm_f):
    b, n, d = x.shape
    depth = norm_g.shape[0]
    assert depth % N_MIXERS == 0, "the final norm is fused into the last Fourier layer"
    last_mlstm = ((depth - 1) // N_MIXERS) * N_MIXERS
    rows = -(-(b + 1) // 8) * 8
    cc = jnp.concatenate([c, c_ctx[None, :], jnp.zeros((rows - b - 1, d), F32)], axis=0)
    mods = _ada(cc, w_ada, b_ada)
    nf = norm_f.reshape(1, d)
    xl, xc = x, ctx
    for i in range(depth):
        j = i // N_MIXERS
        update_ctx = i < last_mlstm
        mod_l = mods[i, :b][:, None, :]
        mod_c = mods[i, b:b + 1][:, None, :]
        gain = norm_g[i].reshape(1, d)
        if i % N_MIXERS == 0:
            p = dict(w_in=m_w_in[j], conv_w=m_conv_w[j], conv_b=m_conv_b[j], w_q=m_w_q[j], w_k=m_w_k[j],
                     w_if=m_w_if[j], b_i=m_b_i[j], b_f=m_b_f[j], ln_w=m_ln_w[j], skip=m_skip[j], w_out=m_w_out[j])
            xl, xc = _mlstm_layer(xl, xc, mod_l, mod_c, gain, p, update_ctx=update_ctx)
        else:
            p = dict(w_in=f_w_in[j], w_out=f_w_out[j])
            xl, xc = _fourier_layer(xl, xc, mod_l, mod_c, gain, p, nf, update_ctx=update_ctx,
                                    final=(i == depth - 1))
    return xl
```

```python
import functools

import numpy as np
import jax
import jax.numpy as jnp
from jax import lax
from jax.experimental import pallas as pl
from jax.experimental.pallas import tpu as pltpu

F32 = jnp.float32
BF16 = jnp.bfloat16

GRID_W = 64
N_MIXERS = 2
MLSTM_HEADS = 4
FOURIER_GROUPS = 8
EPS = 1e-6
SCAN_CHUNK = 256
CONV_SUB = SCAN_CHUNK
GATE_LANES = 128
FFT_SLABS = 8
VMEM_LIMIT = 56 * 1024 * 1024


def _cparams(*sem):
    return pltpu.CompilerParams(dimension_semantics=sem, vmem_limit_bytes=VMEM_LIMIT)


def _sigmoid(v):
    return 1.0 / (1.0 + jnp.exp(-v))


def _silu(v):
    return v * _sigmoid(v)


def _dot(a, b):
    return jnp.dot(a, b, preferred_element_type=F32)


def _dot_nt(a, b):
    return lax.dot_general(a, b, (((1,), (1,)), ((), ())), preferred_element_type=F32)


def _ada_kernel(c_ref, w_ref, b_ref, o_ref):
    s = _silu(c_ref[...])
    o_ref[0] = _dot(s.astype(BF16), w_ref[0].astype(BF16)) + b_ref[0]


def _ada(cc, w_ada, b_ada):
    depth, d, d3 = w_ada.shape
    r = cc.shape[0]
    tn = 512
    return pl.pallas_call(
        _ada_kernel,
        grid=(depth, d3 // tn),
        in_specs=[pl.BlockSpec((r, d), lambda i, j: (0, 0)),
                  pl.BlockSpec((1, d, tn), lambda i, j: (i, 0, j)),
                  pl.BlockSpec((1, 1, tn), lambda i, j: (i, 0, j))],
        out_specs=pl.BlockSpec((1, r, tn), lambda i, j: (i, 0, j)),
        out_shape=jax.ShapeDtypeStruct((depth, r, d3), F32),
        compiler_params=_cparams("parallel", "parallel"),
        name="ada",
    )(cc, w_ada, b_ada.reshape(depth, 1, d3))


def _norm_matmul_kernel(x_ref, mod_ref, g_ref, w_ref, o_ref, *, d):
    x = x_ref[0]
    y = x * lax.rsqrt(jnp.mean(x * x, axis=-1, keepdims=True) + EPS) * g_ref[...]
    mod = mod_ref[0]
    h = y * (1.0 + mod[:, d:2 * d]) + mod[:, :d]
    o_ref[0] = _dot(h.astype(BF16), w_ref[...]).astype(o_ref.dtype)


def _norm_matmul(x, mod, gain, w, *, tm, tn):
    b, n, d = x.shape
    no = w.shape[1]
    per_batch = mod.shape[0] != 1
    mod_map = (lambda j, bi, t: (bi, 0, 0)) if per_batch else (lambda j, bi, t: (0, 0, 0))
    return pl.pallas_call(
        functools.partial(_norm_matmul_kernel, d=d),
        grid=(no // tn, b, n // tm),
        in_specs=[pl.BlockSpec((1, tm, d), lambda j, bi, t: (bi, t, 0)),
                  pl.BlockSpec((1, 1, 3 * d), mod_map),
                  pl.BlockSpec((1, d), lambda j, bi, t: (0, 0)),
                  pl.BlockSpec((d, tn), lambda j, bi, t: (0, j))],
        out_specs=pl.BlockSpec((1, tm, tn), lambda j, bi, t: (bi, t, j)),
        out_shape=jax.ShapeDtypeStruct((b, n, no), BF16),
        compiler_params=_cparams("parallel", "parallel", "parallel"),
        name="norm_matmul",
    )(x, mod, gain, w)


def _conv_qk_kernel(xc_ref, xp_ref, xn_ref, cw_ref, cb_ref, wq_ref, wk_ref, wg_ref,
                    xconv_ref, q_ref, k_ref, kt_ref, gates_ref, *, tm, img_w, has_vert, n_tiles, scale):
    t = pl.program_id(1)
    sub = CONV_SUB
    n_sub = tm // sub
    for c in range(n_sub):
        lo = c * sub
        xcb = xc_ref[0, lo:lo + sub, :]
        xc = xcb.astype(F32)
        if has_vert:
            if c == 0:
                top = jnp.where(t > 0, xp_ref[0].astype(F32), 0.0)
            else:
                top = xc_ref[0, lo - img_w:lo, :].astype(F32)
            if c == n_sub - 1:
                bot = jnp.where(t < n_tiles - 1, xn_ref[0].astype(F32), 0.0)
            else:
                bot = xc_ref[0, lo + sub:lo + sub + img_w, :].astype(F32)
            ext = jnp.concatenate([top, xc, bot], axis=0)
            off = img_w
            taps = (-1, 0, 1)
        else:
            ext = xc
            off = 0
            taps = (0,)
        rows = ext.shape[0]
        col = lax.broadcasted_iota(jnp.int32, (rows, 1), 0) % img_w
        left = jnp.where(col == 0, 0.0, pltpu.roll(ext, 1, 0))
        right = jnp.where(col == img_w - 1, 0.0, pltpu.roll(ext, rows - 1, 0))
        acc = None
        for dr in taps:
            s = off + dr * img_w
            r = 3 * (dr + 1)
            term = (cw_ref[r:r + 1, :] * left[s:s + sub] + cw_ref[r + 1:r + 2, :] * ext[s:s + sub]
                    + cw_ref[r + 2:r + 3, :] * right[s:s + sub])
            acc = term if acc is None else acc + term
        xv = _silu(acc + cb_ref[...]).astype(BF16)
        xconv_ref[0, lo:lo + sub, :] = xv
        qb = _dot(xv, wq_ref[0]).astype(BF16)
        kf = _dot(xv, wk_ref[0]) * scale
        kb = kf.astype(BF16)
        q_ref[0, lo:lo + sub, :] = qb
        k_ref[0, lo:lo + sub, :] = kb
        kt_ref[0, c] = jnp.transpose(kf).astype(BF16)
        gates_ref[0, 0, lo:lo + sub, :] = (_dot(qb, wg_ref[0, 0]) + _dot(kb, wg_ref[0, 1])
                                           + _dot(xcb, wg_ref[0, 2]))


def _conv_qk(a, cw, cb, wq, wk, wg, *, tm, img_w, has_vert):
    b, n, _ = a.shape
    heads, dh, _ = wq.shape
    e = heads * dh
    n_tiles = n // tm
    sub = CONV_SUB
    assert tm % sub == 0
    hb = img_w if has_vert else 16
    per = tm // hb
    last = n // hb - 1
    kern = functools.partial(_conv_qk_kernel, tm=tm, img_w=img_w, has_vert=has_vert, n_tiles=n_tiles,
                             scale=float(dh) ** -0.5)
    tok = lambda bi, t, h: (bi, t, h)
    return pl.pallas_call(
        kern,
        grid=(b, n_tiles, heads),
        in_specs=[pl.BlockSpec((1, tm, dh), tok),
                  pl.BlockSpec((1, hb, dh), lambda bi, t, h: (bi, jnp.maximum(t * per - 1, 0), h)),
                  pl.BlockSpec((1, hb, dh), lambda bi, t, h: (bi, jnp.minimum((t + 1) * per, last), h)),
                  pl.BlockSpec((9, dh), lambda bi, t, h: (0, h)),
                  pl.BlockSpec((1, dh), lambda bi, t, h: (0, h)),
                  pl.BlockSpec((1, dh, dh), lambda bi, t, h: (h, 0, 0)),
                  pl.BlockSpec((1, dh, dh), lambda bi, t, h: (h, 0, 0)),
                  pl.BlockSpec((1, 3, dh, GATE_LANES), lambda bi, t, h: (h, 0, 0, 0))],
        out_specs=[pl.BlockSpec((1, tm, dh), tok),
                   pl.BlockSpec((1, tm, dh), tok),
                   pl.BlockSpec((1, tm, dh), tok),
                   pl.BlockSpec((1, tm // sub, dh, sub), lambda bi, t, h: (bi, t, h, 0)),
                   pl.BlockSpec((1, 1, tm, GATE_LANES), lambda bi, t, h: (bi, h, t, 0))],
        out_shape=[jax.ShapeDtypeStruct((b, n, e), BF16),
                   jax.ShapeDtypeStruct((b, n, e), BF16),
                   jax.ShapeDtypeStruct((b, n, e), BF16),
                   jax.ShapeDtypeStruct((b, n // sub, e, sub), BF16),
                   jax.ShapeDtypeStruct((b, heads, n, GATE_LANES), F32)],
        compiler_params=_cparams("parallel", "parallel", "parallel"),
        name="conv_qk",
    )(a, a, a, cw, cb, wq, wk, wg)


def _split3(a):
    hi = a.astype(BF16)
    r1 = a - hi.astype(F32)
    mid = r1.astype(BF16)
    lo = (r1 - mid.astype(F32)).astype(BF16)
    return hi, mid, lo


def _gate_fin_kernel(g_ref, bias_ref, gcol_ref, bcol_ref, brow_ref, *, L, heads):
    nd = 2 * heads
    row = lax.broadcasted_iota(jnp.int32, (L, L), 0)
    col = lax.broadcasted_iota(jnp.int32, (L, L), 1)
    tri_l = (col <= row).astype(BF16)
    tri_u = (col >= row).astype(BF16)
    lane = lax.broadcasted_iota(jnp.int32, (L, GATE_LANES), 1)
    for c0 in range(0, g_ref.shape[2], L):
        rows = slice(c0, c0 + L)
        pre = jnp.sum(g_ref[0, :, rows, :], axis=0) + bias_ref[...]
        lf = jnp.minimum(pre, 0.0) - jnp.log(1.0 + jnp.exp(-jnp.abs(pre)))
        parts = _split3(lf)
        g_l = _dot(tri_l, parts[0]) + _dot(tri_l, parts[1]) + _dot(tri_l, parts[2])
        g_u = _dot(tri_u, parts[0]) + _dot(tri_u, parts[1]) + _dot(tri_u, parts[2])
        g_sel = jnp.where(lane < nd + heads, g_l, g_u)
        g = pltpu.roll(g_sel, GATE_LANES - nd, 1)
        bc = pre - g
        gcol_ref[0, rows, :] = g
        bcol_ref[0, rows, :] = bc
        brow_ref[0, :, rows] = jnp.transpose(bc)[0:8, :]


def _gate_fin(gates, bias, *, heads):
    b, _, n, _ = gates.shape
    L = SCAN_CHUNK
    assert 2 * heads == 8
    tr = min(n, 4 * L)
    blk = pl.BlockSpec((1, tr, GATE_LANES), lambda bi, c: (bi, c, 0))
    return pl.pallas_call(
        functools.partial(_gate_fin_kernel, L=L, heads=heads),
        grid=(b, n // tr),
        in_specs=[pl.BlockSpec((1, heads, tr, GATE_LANES), lambda bi, c: (bi, 0, c, 0)),
                  pl.BlockSpec((1, GATE_LANES), lambda bi, c: (0, 0))],
        out_specs=[blk, blk, pl.BlockSpec((1, 8, tr), lambda bi, c: (bi, 0, c))],
        out_shape=[jax.ShapeDtypeStruct((b, n, GATE_LANES), F32),
                   jax.ShapeDtypeStruct((b, n, GATE_LANES), F32),
                   jax.ShapeDtypeStruct((b, 8, n), F32)],
        compiler_params=_cparams("parallel", "parallel"),
        name="gate_fin",
    )(gates, bias)


def _chain_step(q_ref, k_ref, kt_ref, v_ref, gc_ref, bc_ref, br_ref, out_ref, ct_ref, n_ref, m_ref,
                *, slot, head, lane_idx, lower, L, dh):
    hs = slice(head * dh, (head + 1) * dh)
    q = q_ref[0, :, hs]
    kt = kt_ref[0, 0, hs, :]
    v = v_ref[0, :, hs]
    gcol = gc_ref[0, :, lane_idx:lane_idx + 1]
    bcol = bc_ref[0, :, lane_idx:lane_idx + 1]
    brow = br_ref[0, lane_idx:lane_idx + 1, :]
    m_prev = m_ref[slot][:, 0:1]

    row = lax.broadcasted_iota(jnp.int32, (L, L), 0)
    col = lax.broadcasted_iota(jnp.int32, (L, L), 1)
    mask = (col <= row) if lower else (col >= row)
    logw = jnp.where(mask, gcol + brow, -jnp.inf)
    log_prev = gcol + m_prev
    m_new = jnp.maximum(log_prev, jnp.max(logw, axis=1, keepdims=True))
    w_intra = jnp.exp(logw - m_new)
    w_prev = jnp.exp(log_prev - m_new)

    if out_ref is not None:
        p = _dot(q, kt) * w_intra
        n_row = n_ref[slot]
        qn = jnp.sum(q.astype(F32) * n_row, axis=1, keepdims=True)
        den = jnp.sum(p, axis=1, keepdims=True) + w_prev * qn
        num = _dot(p.astype(BF16), v) + w_prev * _dot(q, ct_ref[slot].astype(BF16))
        inv = 1.0 / jnp.maximum(jnp.abs(den), jnp.exp(-m_new))
        out_ref[0, :, hs] = (num * inv).astype(out_ref.dtype)

    rowv = lax.broadcasted_iota(jnp.int32, (L, 1), 0)
    last = (rowv == (L - 1 if lower else 0))
    g_last = jnp.sum(jnp.where(last, gcol, 0.0), axis=0, keepdims=True)
    m_end = jnp.sum(jnp.where(last, m_new, 0.0), axis=0, keepdims=True)
    w_src = jnp.exp(bcol + g_last - m_end)
    decay = jnp.exp(g_last + m_prev - m_end)
    vw = (w_src * v.astype(F32)).astype(BF16)
    ct_ref[slot] = decay * ct_ref[slot] + _dot(kt, vw)
    n_ref[slot] = decay * n_ref[slot] + jnp.sum(w_src * k_ref[0, :, hs].astype(F32), axis=0, keepdims=True)
    m_ref[slot] = jnp.broadcast_to(m_end, m_ref.shape[1:])


def _scan_kernel(*refs, L, heads, dh, emit_ctx):
    ctx_in = refs[0:7]
    fwd_in = refs[7:14]
    rev_in = refs[14:21]
    hf_ref, hr_ref = refs[21:23]
    if emit_ctx:
        hcf_ref, hcr_ref = refs[23:25]
        ct_ref, n_ref, m_ref = refs[25:28]
    else:
        hcf_ref = hcr_ref = None
        ct_ref, n_ref, m_ref = refs[23:26]
    s = pl.program_id(1)
    step = functools.partial(_chain_step, ct_ref=ct_ref, n_ref=n_ref, m_ref=m_ref, L=L, dh=dh)

    @pl.when(s == 0)
    def _():
        ct_ref[...] = jnp.zeros_like(ct_ref)
        n_ref[...] = jnp.zeros_like(n_ref)
        m_ref[...] = jnp.zeros_like(m_ref)
        for h in range(heads):
            step(*ctx_in, hcf_ref, slot=h, head=h, lane_idx=h, lower=True)
            step(*ctx_in, hcr_ref, slot=heads + h, head=h, lane_idx=heads + h, lower=False)

    @pl.when(s > 0)
    def _():
        for h in range(heads):
            step(*fwd_in, hf_ref, slot=h, head=h, lane_idx=h, lower=True)
            step(*rev_in, hr_ref, slot=heads + h, head=h, lane_idx=heads + h, lower=False)


def _scan(lat, ctx, *, heads, emit_ctx):
    b, n, e = lat["q"].shape
    nc = ctx["q"].shape[1]
    L = SCAN_CHUNK
    assert nc == L, "the context prefix is processed as exactly one chunk"
    dh = e // heads
    n_lat = n // L

    def specs(imap):
        tok = lambda bi, s: (bi, imap(s), 0)
        return [pl.BlockSpec((1, L, e), tok),
                pl.BlockSpec((1, L, e), tok),
                pl.BlockSpec((1, 1, e, L), lambda bi, s: (bi, imap(s), 0, 0)),
                pl.BlockSpec((1, L, e), tok),
                pl.BlockSpec((1, L, GATE_LANES), tok),
                pl.BlockSpec((1, L, GATE_LANES), tok),
                pl.BlockSpec((1, 8, L), lambda bi, s: (bi, 0, imap(s)))]

    def args(d):
        return [d["q"], d["k"], d["kt"], d["a"], d["gcol"], d["bcol"], d["brow"]]

    ctx_map = lambda s: 0
    fwd_map = lambda s: jnp.maximum(s - 1, 0)
    rev_map = lambda s: n_lat - jnp.maximum(s, 1)
    out_specs = [pl.BlockSpec((1, L, e), lambda bi, s: (bi, fwd_map(s), 0)),
                 pl.BlockSpec((1, L, e), lambda bi, s: (bi, rev_map(s), 0))]
    out_shape = [jax.ShapeDtypeStruct((b, n, e), BF16), jax.ShapeDtypeStruct((b, n, e), BF16)]
    if emit_ctx:
        out_specs += [pl.BlockSpec((1, L, e), lambda bi, s: (bi, 0, 0))] * 2
        out_shape += [jax.ShapeDtypeStruct((b, nc, e), BF16)] * 2
    return pl.pallas_call(
        functools.partial(_scan_kernel, L=L, heads=heads, dh=dh, emit_ctx=emit_ctx),
        grid=(b, n_lat + 1),
        in_specs=specs(ctx_map) + specs(fwd_map) + specs(rev_map),
        out_specs=out_specs,
        out_shape=out_shape,
        scratch_shapes=[pltpu.VMEM((2 * heads, dh, dh), F32),
                        pltpu.VMEM((2 * heads, 1, dh), F32),
                        pltpu.VMEM((2 * heads, 1, GATE_LANES), F32)],
        compiler_params=_cparams("parallel", "arbitrary"),
        name="mlstm_scan",
    )(*(args(ctx) + args(lat) + args(lat)))


def _mlstm_out_kernel(hf_ref, hr_ref, o_ref, z_ref, xc_ref, x_ref, lnw_ref, skip_ref, w_ref, mod_ref, out_ref,
                      *, d, dh, heads):
    hs = hf_ref[0].astype(F32) + hr_ref[0].astype(F32)
    parts = []
    for hh in range(heads):
        seg = hs[:, hh * dh:(hh + 1) * dh]
        cen = seg - jnp.mean(seg, axis=-1, keepdims=True)
        var = jnp.mean(cen * cen, axis=-1, keepdims=True)
        parts.append(cen * lax.rsqrt(var + EPS))
    hn = jnp.concatenate(parts, axis=1) * lnw_ref[...]
    hn = _sigmoid(o_ref[0].astype(F32)) * hn
    y = (hn + skip_ref[...] * xc_ref[0].astype(F32)) * _silu(z_ref[0].astype(F32))
    yo = _dot(y.astype(BF16), w_ref[...])
    out_ref[0] = x_ref[0] + mod_ref[0][:, 2 * d:3 * d] * yo


def _mlstm_out(hf, hr, a, xconv, x, lnw, skip, w, mod, *, tm, heads):
    b, n, d = x.shape
    e = hf.shape[2]
    per_batch = mod.shape[0] != 1
    mod_map = (lambda bi, t: (bi, 0, 0)) if per_batch else (lambda bi, t: (0, 0, 0))
    tok = lambda bi, t: (bi, t, 0)
    vec = pl.BlockSpec((1, e), lambda bi, t: (0, 0))
    return pl.pallas_call(
        functools.partial(_mlstm_out_kernel, d=d, dh=e // heads, heads=heads),
        grid=(b, n // tm),
        in_specs=[pl.BlockSpec((1, tm, e), tok),
                  pl.BlockSpec((1, tm, e), tok),
                  pl.BlockSpec((1, tm, e), lambda bi, t: (bi, t, 1)),
                  pl.BlockSpec((1, tm, e), lambda bi, t: (bi, t, 2)),
                  pl.BlockSpec((1, tm, e), tok),
                  pl.BlockSpec((1, tm, d), tok),
                  vec, vec,
                  pl.BlockSpec((e, d), lambda bi, t: (0, 0)),
                  pl.BlockSpec((1, 1, 3 * d), mod_map)],
        out_specs=pl.BlockSpec((1, tm, d), tok),
        out_shape=jax.ShapeDtypeStruct((b, n, d), F32),
        compiler_params=_cparams("parallel", "parallel"),
        name="mlstm_out",
    )(hf, hr, a, a, xconv, x, lnw, skip, w, mod)


def _chan_dft_kernel(u_ref, w_ref, a_ref, b_ref, *, groups, gd):
    for g in range(groups):
        r = _dot(u_ref[0, :, g * gd:(g + 1) * gd], w_ref[...])
        a_ref[0, :, g * gd:(g + 1) * gd] = r[:, :gd].astype(BF16)
        b_ref[0, :, g * gd:(g + 1) * gd] = r[:, gd:].astype(BF16)


def _chan_dft(uz, wc, *, tm, groups):
    b, n, e2 = uz.shape
    e = e2 // 2
    gd = e // groups
    tok = lambda bi, t: (bi, t, 0)
    return pl.pallas_call(
        functools.partial(_chan_dft_kernel, groups=groups, gd=gd),
        grid=(b, n // tm),
        in_specs=[pl.BlockSpec((1, tm, e), tok), pl.BlockSpec((gd, 2 * gd), lambda bi, t: (0, 0))],
        out_specs=[pl.BlockSpec((1, tm, e), tok)] * 2,
        out_shape=[jax.ShapeDtypeStruct((b, n, e), BF16)] * 2,
        compiler_params=_cparams("parallel", "parallel"),
        name="chan_dft",
    )(uz, wc)


def _pos_dft_kernel(cn_ref, sn_ref, a_ref, b_ref, y_ref):
    y_ref[0] = (_dot(cn_ref[...], a_ref[0]) - _dot(sn_ref[...], b_ref[0])).astype(y_ref.dtype)


def _pos_dft(cn, sn, a, bm, *, tm, tn):
    b, n, e = a.shape
    return pl.pallas_call(
        _pos_dft_kernel,
        grid=(b, n // tm, e // tn),
        in_specs=[pl.BlockSpec((tm, n), lambda bi, i, j: (i, 0)),
                  pl.BlockSpec((tm, n), lambda bi, i, j: (i, 0)),
                  pl.BlockSpec((1, n, tn), lambda bi, i, j: (bi, 0, j)),
                  pl.BlockSpec((1, n, tn), lambda bi, i, j: (bi, 0, j))],
        out_specs=pl.BlockSpec((1, tm, tn), lambda bi, i, j: (bi, i, j)),
        out_shape=jax.ShapeDtypeStruct((b, n, e), BF16),
        compiler_params=_cparams("parallel", "parallel", "parallel"),
        name="pos_dft",
    )(cn, sn, a, bm)


def _small_dft(zs):
    n = len(zs)
    if n == 1:
        return zs
    even = _small_dft(zs[0::2])
    odd = _small_dft(zs[1::2])
    out = [None] * n
    for k in range(n // 2):
        o_re, o_im = odd[k]
        if k == 0:
            t_re, t_im = o_re, o_im
        elif 4 * k == n:
            t_re, t_im = o_im, -o_re
        else:
            c = float(np.cos(2.0 * np.pi * k / n))
            s = float(np.sin(2.0 * np.pi * k / n))
            t_re, t_im = c * o_re + s * o_im, c * o_im - s * o_re
        e_re, e_im = even[k]
        out[k] = (e_re + t_re, e_im + t_im)
        out[k + n // 2] = (e_re - t_re, e_im - t_im)
    return out


def _fft_a_kernel(u_ref, wc_ref, tc_ref, ts_ref, gr_ref, gi_ref, *, n1, gd):
    zs = []
    for a in range(n1):
        r = _dot(u_ref[0, a], wc_ref[...])
        zs.append((r[:, :gd], r[:, gd:]))
    gs = _small_dft(zs)
    for k1 in range(n1):
        re, im = gs[k1]
        if k1 > 0:
            c = tc_ref[k1]
            s = ts_ref[k1]
            re, im = re * c + im * s, im * c - re * s
        gr_ref[0, k1] = re.astype(BF16)
        gi_ref[0, k1] = im.astype(BF16)


def _fft_a(uz, wc, tc, ts, *, n1, groups, tr):
    b, n, e2 = uz.shape
    e = e2 // 2
    gd = e // groups
    n2 = n // n1
    blk = pl.BlockSpec((1, n1, tr, gd), lambda bi, r, g: (bi, 0, r, g))
    tw = pl.BlockSpec((n1, tr, gd), lambda bi, r, g: (0, r, 0))
    return pl.pallas_call(
        functools.partial(_fft_a_kernel, n1=n1, gd=gd),
        grid=(b, n2 // tr, groups),
        in_specs=[blk, pl.BlockSpec((gd, 2 * gd), lambda bi, r, g: (0, 0)), tw, tw],
        out_specs=[blk, blk],
        out_shape=[jax.ShapeDtypeStruct((b, n1, n2, e), BF16)] * 2,
        compiler_params=_cparams("parallel", "parallel", "parallel"),
        name="fft_a",
    )(uz.reshape(b, n1, n2, e2), wc, tc, ts)


def _fft_b_kernel(wc_ref, ws_ref, gr_ref, gi_ref, y_ref, scr_ref, *, n1, n2):
    for k1 in range(n1):
        yk = _dot(wc_ref[...], gr_ref[0, k1]) + _dot(ws_ref[...], gi_ref[0, k1])
        for c in range(scr_ref.shape[0]):
            scr_ref[c, pl.ds(k1, n2, stride=n1), :] = yk[:, c * 128:(c + 1) * 128]
    for c in range(scr_ref.shape[0]):
        y_ref[0, :, c * 128:(c + 1) * 128] = scr_ref[c].astype(y_ref.dtype)


def _fft_b(w2c, w2s, gr, gi, *, tn):
    b, n1, n2, e = gr.shape
    blk = pl.BlockSpec((1, n1, n2, tn), lambda bi, j: (bi, 0, 0, j))
    wsp = pl.BlockSpec((n2, n2), lambda bi, j: (0, 0))
    return pl.pallas_call(
        functools.partial(_fft_b_kernel, n1=n1, n2=n2),
        grid=(b, e // tn),
        in_specs=[wsp, wsp, blk, blk],
        out_specs=pl.BlockSpec((1, n1 * n2, tn), lambda bi, j: (bi, 0, j)),
        out_shape=jax.ShapeDtypeStruct((b, n1 * n2, e), BF16),
        scratch_shapes=[pltpu.VMEM((tn // 128, n1 * n2, 128), F32)],
        compiler_params=_cparams("parallel", "parallel"),
        name="fft_b",
    )(w2c, w2s, gr, gi)


def _fnet_out_kernel(y_ref, z_ref, x_ref, w_ref, mod_ref, nf_ref, out_ref, *, d, final):
    y = y_ref[0].astype(F32) * _silu(z_ref[0].astype(F32))
    yo = _dot(y.astype(BF16), w_ref[...])
    xn = x_ref[0] + mod_ref[0][:, 2 * d:3 * d] * yo
    if final:
        xn = xn * lax.rsqrt(jnp.mean(xn * xn, axis=-1, keepdims=True) + EPS) * nf_ref[...]
    out_ref[0] = xn


def _fnet_out(y, uz, x, w, mod, nf, *, tm, final):
    b, n, d = x.shape
    e = y.shape[2]
    per_batch = mod.shape[0] != 1
    mod_map = (lambda bi, t: (bi, 0, 0)) if per_batch else (lambda bi, t: (0, 0, 0))
    tok = lambda bi, t: (bi, t, 0)
    return pl.pallas_call(
        functools.partial(_fnet_out_kernel, d=d, final=final),
        grid=(b, n // tm),
        in_specs=[pl.BlockSpec((1, tm, e), tok),
                  pl.BlockSpec((1, tm, e), lambda bi, t: (bi, t, 1)),
                  pl.BlockSpec((1, tm, d), tok),
                  pl.BlockSpec((e, d), lambda bi, t: (0, 0)),
                  pl.BlockSpec((1, 1, 3 * d), mod_map),
                  pl.BlockSpec((1, d), lambda bi, t: (0, 0))],
        out_specs=pl.BlockSpec((1, tm, d), tok),
        out_shape=jax.ShapeDtypeStruct((b, n, d), F32),
        compiler_params=_cparams("parallel", "parallel"),
        name="fnet_out",
    )(y, uz, x, w, mod, nf)


def _dft_mats(n, norm):
    k = lax.broadcasted_iota(jnp.int32, (n, n), 0)
    m = lax.broadcasted_iota(jnp.int32, (n, n), 1)
    ang = ((k * m) % n).astype(F32) * (2.0 * np.pi / n)
    return (jnp.cos(ang) * norm).astype(BF16), (jnp.sin(ang) * norm).astype(BF16)


def _mlstm_layer(xl, xc, mod_l, mod_c, gain, p, *, update_ctx):
    heads = MLSTM_HEADS
    n = xl.shape[1]
    nc = xc.shape[1]
    w_in = p["w_in"].astype(BF16)
    e = w_in.shape[1] // 3
    dh = e // heads
    cw = p["conv_w"].reshape(9, e)
    cb = p["conv_b"].reshape(1, e)
    wq = p["w_q"].astype(BF16)
    wk = p["w_k"].astype(BF16)
    wif = p["w_if"].reshape(2, 3, heads, dh, 2, heads)
    wg = jnp.transpose(wif, (2, 1, 3, 4, 0, 5)).reshape(heads, 3, dh, 4 * heads)
    wg = jnp.pad(wg, ((0, 0), (0, 0), (0, 0), (0, GATE_LANES - 4 * heads))).astype(BF16)
    bias = jnp.concatenate([p["b_i"].reshape(-1), p["b_f"].reshape(-1)])
    bias = jnp.pad(bias, (0, GATE_LANES - 4 * heads)).reshape(1, GATE_LANES)
    lnw = p["ln_w"].reshape(1, e)
    skip = p["skip"].reshape(1, e)
    w_out = p["w_out"].astype(BF16)

    def branch(x, mod, tm, img_w, has_vert):
        a = _norm_matmul(x, mod, gain, w_in, tm=tm, tn=e)
        xconv, q, k, kt, gates = _conv_qk(a, cw, cb, wq, wk, wg, tm=tm, img_w=img_w, has_vert=has_vert)
        gcol, bcol, brow = _gate_fin(gates, bias, heads=heads)
        return dict(a=a, xconv=xconv, q=q, k=k, kt=kt, gcol=gcol, bcol=bcol, brow=brow)

    lat = branch(xl, mod_l, min(1024, n), GRID_W, True)
    ctx = branch(xc, mod_c, nc, nc, False)
    outs = _scan(lat, ctx, heads=heads, emit_ctx=update_ctx)
    xl = _mlstm_out(outs[0], outs[1], lat["a"], lat["xconv"], xl, lnw, skip, w_out, mod_l,
                    tm=min(256, n), heads=heads)
    if update_ctx:
        xc = _mlstm_out(outs[2], outs[3], ctx["a"], ctx["xconv"], xc, lnw, skip, w_out, mod_c,
                        tm=nc, heads=heads)
    return xl, xc


def _fourier_layer(xl, xc, mod_l, mod_c, gain, p, nf, *, update_ctx, final):
    groups = FOURIER_GROUPS
    n = xl.shape[1]
    nc = xc.shape[1]
    w_in = p["w_in"].astype(BF16)
    w_out = p["w_out"].astype(BF16)
    e = w_in.shape[1] // 2
    gd = e // groups
    cc, sc = _dft_mats(gd, gd ** -0.5)
    wc = jnp.concatenate([cc, sc], axis=1)

    def position_dft(uz, m):
        n1 = FFT_SLABS if m % (FFT_SLABS * 256) == 0 else 1
        if n1 == 1:
            a, bm = _chan_dft(uz, wc, tm=min(1024, m), groups=groups)
            cn, sn = _dft_mats(m, m ** -0.5)
            return _pos_dft(cn, sn, a, bm, tm=min(512, m), tn=min(512, e))
        n2 = m // n1
        k1 = lax.broadcasted_iota(jnp.int32, (n1, n2, gd), 0)
        p2 = lax.broadcasted_iota(jnp.int32, (n1, n2, gd), 1)
        ang = ((k1 * p2) % m).astype(F32) * (2.0 * np.pi / m)
        gr, gi = _fft_a(uz, jnp.concatenate([cc, -sc], axis=1), jnp.cos(ang), jnp.sin(ang),
                        n1=n1, groups=groups, tr=min(256, n2))
        w2c, w2s = _dft_mats(n2, m ** -0.5)
        return _fft_b(w2c, w2s, gr, gi, tn=256)

    def branch(x, mod, tm, fin):
        m = x.shape[1]
        uz = _norm_matmul(x, mod, gain, w_in, tm=tm, tn=e)
        y = position_dft(uz, m)
        return _fnet_out(y, uz, x, w_out, mod, nf, tm=min(512, m), final=fin)

    xl = branch(xl, mod_l, min(1024, n), final)
    if update_ctx:
        xc = branch(xc, mod_c, nc, False)
    return xl, xc


def kernel(x, c, ctx, c_ctx, norm_g, w_ada, b_ada, m_w_in, m_conv_w, m_conv_b, m_w_q, m_w_k, m_w_if, m_b_i, m_b_f,
           m_ln_w, m_skip, m_w_out, f_w_in, f_w_out, norm_f):
    b, n, d = x.shape
    depth = norm_g.shape[0]
    assert depth % N_MIXERS == 0, "the final norm is fused into the last Fourier layer"
    last_mlstm = ((depth - 1) // N_MIXERS) * N_MIXERS
    rows = -(-(b + 1) // 8) * 8
    cc = jnp.concatenate([c, c_ctx[None, :], jnp.zeros((rows - b - 1, d), F32)], axis=0)
    mods = _ada(cc, w_ada, b_ada)
    nf = norm_f.reshape(1, d)
    xl, xc = x, ctx
    for i in range(depth):
        j = i // N_MIXERS
        update_ctx = i < last_mlstm
        mod_l = mods[i, :b][:, None, :]
        mod_c = mods[i, b:b + 1][:, None, :]
        gain = norm_g[i].reshape(1, d)
        if i % N_MIXERS == 0:
            p = dict(w_in=m_w_in[j], conv_w=m_conv_w[j], conv_b=m_conv_b[j], w_q=m_w_q[j], w_k=m_w_k[j],
                     w_if=m_w_if[j], b_i=m_b_i[j], b_f=m_b_f[j], ln_w=m_ln_w[j], skip=m_skip[j], w_out=m_w_out[j])
            xl, xc = _mlstm_layer(xl, xc, mod_l, mod_c, gain, p, update_ctx=update_ctx)
        else:
            p = dict(w_in=f_w_in[j], w_out=f_w_out[j])
            xl, xc = _fourier_layer(xl, xc, mod_l, mod_c, gain, p, nf, update_ctx=update_ctx,
                                    final=(i == depth - 1))
    return xl
```

```python
import functools

import numpy as np
import jax
import jax.numpy as jnp
from jax import lax
from jax.experimental import pallas as pl
from jax.experimental.pallas import tpu as pltpu

F32 = jnp.float32
BF16 = jnp.bfloat16

GRID_W = 64
N_MIXERS = 2
MLSTM_HEADS = 4
FOURIER_GROUPS = 8
EPS = 1e-6
SCAN_CHUNK = 256
CONV_SUB = SCAN_CHUNK
GATE_LANES = 128
FFT_SLABS = 8
VMEM_LIMIT = 56 * 1024 * 1024


def _cparams(*sem):
    return pltpu.CompilerParams(dimension_semantics=sem, vmem_limit_bytes=VMEM_LIMIT)


def _sigmoid(v):
    return 1.0 / (1.0 + jnp.exp(-v))


def _silu(v):
    return v * _sigmoid(v)


def _dot(a, b):
    return jnp.dot(a, b, preferred_element_type=F32)


def _ada_kernel(c_ref, w_ref, b_ref, o_ref):
    s = _silu(c_ref[...])
    o_ref[0] = _dot(s.astype(BF16), w_ref[0].astype(BF16)) + b_ref[0]


def _ada(cc, w_ada, b_ada):
    depth, d, d3 = w_ada.shape
    r = cc.shape[0]
    tn = 512
    return pl.pallas_call(
        _ada_kernel,
        grid=(depth, d3 // tn),
        in_specs=[pl.BlockSpec((r, d), lambda i, j: (0, 0)),
                  pl.BlockSpec((1, d, tn), lambda i, j: (i, 0, j)),
                  pl.BlockSpec((1, 1, tn), lambda i, j: (i, 0, j))],
        out_specs=pl.BlockSpec((1, r, tn), lambda i, j: (i, 0, j)),
        out_shape=jax.ShapeDtypeStruct((depth, r, d3), F32),
        compiler_params=_cparams("parallel", "parallel"),
        name="ada",
    )(cc, w_ada, b_ada.reshape(depth, 1, d3))


def _norm_matmul_kernel(x_ref, mod_ref, g_ref, w_ref, o_ref, *, d):
    x = x_ref[0]
    y = x * lax.rsqrt(jnp.mean(x * x, axis=-1, keepdims=True) + EPS) * g_ref[...]
    mod = mod_ref[0]
    h = y * (1.0 + mod[:, d:2 * d]) + mod[:, :d]
    o_ref[0] = _dot(h.astype(BF16), w_ref[...]).astype(o_ref.dtype)


def _norm_matmul(x, mod, gain, w, *, tm, tn):
    b, n, d = x.shape
    no = w.shape[1]
    per_batch = mod.shape[0] != 1
    mod_map = (lambda j, bi, t: (bi, 0, 0)) if per_batch else (lambda j, bi, t: (0, 0, 0))
    return pl.pallas_call(
        functools.partial(_norm_matmul_kernel, d=d),
        grid=(no // tn, b, n // tm),
        in_specs=[pl.BlockSpec((1, tm, d), lambda j, bi, t: (bi, t, 0)),
                  pl.BlockSpec((1, 1, 3 * d), mod_map),
                  pl.BlockSpec((1, d), lambda j, bi, t: (0, 0)),
                  pl.BlockSpec((d, tn), lambda j, bi, t: (0, j))],
        out_specs=pl.BlockSpec((1, tm, tn), lambda j, bi, t: (bi, t, j)),
        out_shape=jax.ShapeDtypeStruct((b, n, no), BF16),
        compiler_params=_cparams("parallel", "parallel", "parallel"),
        name="norm_matmul",
    )(x, mod, gain, w)


def _conv_qk_kernel(xc_ref, xp_ref, xn_ref, cw_ref, cb_ref, wq_ref, wk_ref, wg_ref,
                    xconv_ref, q_ref, k_ref, kt_ref, gates_ref, *, tm, img_w, has_vert, n_tiles, scale):
    t = pl.program_id(1)
    sub = CONV_SUB
    n_sub = tm // sub
    for c in range(n_sub):
        lo = c * sub
        xcb = xc_ref[0, lo:lo + sub, :]
        xc = xcb.astype(F32)
        if has_vert:
            if c == 0:
                top = jnp.where(t > 0, xp_ref[0].astype(F32), 0.0)
            else:
                top = xc_ref[0, lo - img_w:lo, :].astype(F32)
            if c == n_sub - 1:
                bot = jnp.where(t < n_tiles - 1, xn_ref[0].astype(F32), 0.0)
            else:
                bot = xc_ref[0, lo + sub:lo + sub + img_w, :].astype(F32)
            ext = jnp.concatenate([top, xc, bot], axis=0)
            off = img_w
            taps = (-1, 0, 1)
        else:
            ext = xc
            off = 0
            taps = (0,)
        rows = ext.shape[0]
        col = lax.broadcasted_iota(jnp.int32, (rows, 1), 0) % img_w
        left = jnp.where(col == 0, 0.0, pltpu.roll(ext, 1, 0))
        right = jnp.where(col == img_w - 1, 0.0, pltpu.roll(ext, rows - 1, 0))
        acc = None
        for dr in taps:
            s = off + dr * img_w
            r = 3 * (dr + 1)
            term = (cw_ref[r:r + 1, :] * left[s:s + sub] + cw_ref[r + 1:r + 2, :] * ext[s:s + sub]
                    + cw_ref[r + 2:r + 3, :] * right[s:s + sub])
            acc = term if acc is None else acc + term
        xv = _silu(acc + cb_ref[...]).astype(BF16)
        xconv_ref[0, lo:lo + sub, :] = xv
        qb = _dot(xv, wq_ref[0]).astype(BF16)
        kf = _dot(xv, wk_ref[0]) * scale
        kb = kf.astype(BF16)
        q_ref[0, lo:lo + sub, :] = qb
        k_ref[0, lo:lo + sub, :] = kb
        kt_ref[0, c] = jnp.transpose(kf).astype(BF16)
        gates_ref[0, 0, lo:lo + sub, :] = (_dot(qb, wg_ref[0, 0]) + _dot(kb, wg_ref[0, 1])
                                           + _dot(xcb, wg_ref[0, 2]))


def _conv_qk(a, cw, cb, wq, wk, wg, *, tm, img_w, has_vert):
    b, n, _ = a.shape
    heads, dh, _ = wq.shape
    e = heads * dh
    n_tiles = n // tm
    sub = CONV_SUB
    assert tm % sub == 0
    hb = img_w if has_vert else 16
    per = tm // hb
    last = n // hb - 1
    kern = functools.partial(_conv_qk_kernel, tm=tm, img_w=img_w, has_vert=has_vert, n_tiles=n_tiles,
                             scale=float(dh) ** -0.5)
    tok = lambda bi, t, h: (bi, t, h)
    return pl.pallas_call(
        kern,
        grid=(b, n_tiles, heads),
        in_specs=[pl.BlockSpec((1, tm, dh), tok),
                  pl.BlockSpec((1, hb, dh), lambda bi, t, h: (bi, jnp.maximum(t * per - 1, 0), h)),
                  pl.BlockSpec((1, hb, dh), lambda bi, t, h: (bi, jnp.minimum((t + 1) * per, last), h)),
                  pl.BlockSpec((9, dh), lambda bi, t, h: (0, h)),
                  pl.BlockSpec((1, dh), lambda bi, t, h: (0, h)),
                  pl.BlockSpec((1, dh, dh), lambda bi, t, h: (h, 0, 0)),
                  pl.BlockSpec((1, dh, dh), lambda bi, t, h: (h, 0, 0)),
                  pl.BlockSpec((1, 3, dh, GATE_LANES), lambda bi, t, h: (h, 0, 0, 0))],
        out_specs=[pl.BlockSpec((1, tm, dh), tok),
                   pl.BlockSpec((1, tm, dh), tok),
                   pl.BlockSpec((1, tm, dh), tok),
                   pl.BlockSpec((1, tm // sub, dh, sub), lambda bi, t, h: (bi, t, h, 0)),
                   pl.BlockSpec((1, 1, tm, GATE_LANES), lambda bi, t, h: (bi, h, t, 0))],
        out_shape=[jax.ShapeDtypeStruct((b, n, e), BF16),
                   jax.ShapeDtypeStruct((b, n, e), BF16),
                   jax.ShapeDtypeStruct((b, n, e), BF16),
                   jax.ShapeDtypeStruct((b, n // sub, e, sub), BF16),
                   jax.ShapeDtypeStruct((b, heads, n, GATE_LANES), F32)],
        compiler_params=_cparams("parallel", "parallel", "parallel"),
        name="conv_qk",
    )(a, a, a, cw, cb, wq, wk, wg)


def _split3(a):
    hi = a.astype(BF16)
    r1 = a - hi.astype(F32)
    mid = r1.astype(BF16)
    lo = (r1 - mid.astype(F32)).astype(BF16)
    return hi, mid, lo


def _gate_fin_kernel(g_ref, bias_ref, gcol_ref, bcol_ref, brow_ref, *, L, heads):
    nd = 2 * heads
    row = lax.broadcasted_iota(jnp.int32, (L, L), 0)
    col = lax.broadcasted_iota(jnp.int32, (L, L), 1)
    tri_l = (col <= row).astype(BF16)
    tri_u = (col >= row).astype(BF16)
    lane = lax.broadcasted_iota(jnp.int32, (L, GATE_LANES), 1)
    for c0 in range(0, g_ref.shape[2], L):
        rows = slice(c0, c0 + L)
        pre = jnp.sum(g_ref[0, :, rows, :], axis=0) + bias_ref[...]
        lf = jnp.minimum(pre, 0.0) - jnp.log(1.0 + jnp.exp(-jnp.abs(pre)))
        parts = _split3(lf)
        g_l = _dot(tri_l, parts[0]) + _dot(tri_l, parts[1]) + _dot(tri_l, parts[2])
        g_u = _dot(tri_u, parts[0]) + _dot(tri_u, parts[1]) + _dot(tri_u, parts[2])
        g_sel = jnp.where(lane < nd + heads, g_l, g_u)
        g = pltpu.roll(g_sel, GATE_LANES - nd, 1)
        bc = pre - g
        gcol_ref[0, rows, :] = g
        bcol_ref[0, rows, :] = bc
        brow_ref[0, :, rows] = jnp.transpose(bc)[0:8, :]


def _gate_fin(gates, bias, *, heads):
    b, _, n, _ = gates.shape
    L = SCAN_CHUNK
    assert 2 * heads == 8
    tr = min(n, 4 * L)
    blk = pl.BlockSpec((1, tr, GATE_LANES), lambda bi, c: (bi, c, 0))
    return pl.pallas_call(
        functools.partial(_gate_fin_kernel, L=L, heads=heads),
        grid=(b, n // tr),
        in_specs=[pl.BlockSpec((1, heads, tr, GATE_LANES), lambda bi, c: (bi, 0, c, 0)),
                  pl.BlockSpec((1, GATE_LANES), lambda bi, c: (0, 0))],
        out_specs=[blk, blk, pl.BlockSpec((1, 8, tr), lambda bi, c: (bi, 0, c))],
        out_shape=[jax.ShapeDtypeStruct((b, n, GATE_LANES), F32),
                   jax.ShapeDtypeStruct((b, n, GATE_LANES), F32),
                   jax.ShapeDtypeStruct((b, 8, n), F32)],
        compiler_params=_cparams("parallel", "parallel"),
        name="gate_fin",
    )(gates, bias)


def _chain_step(q_ref, k_ref, kt_ref, v_ref, gc_ref, bc_ref, br_ref, out_ref, ct_ref, n_ref, m_ref,
                *, slot, head, lane_idx, lower, L, dh):
    hs = slice(head * dh, (head + 1) * dh)
    q = q_ref[0, :, hs]
    kt = kt_ref[0, 0, hs, :]
    v = v_ref[0, :, hs]
    gcol = gc_ref[0, :, lane_idx:lane_idx + 1]
    bcol = bc_ref[0, :, lane_idx:lane_idx + 1]
    brow = br_ref[0, lane_idx:lane_idx + 1, :]
    m_prev = m_ref[slot][:, 0:1]

    row = lax.broadcasted_iota(jnp.int32, (L, L), 0)
    col = lax.broadcasted_iota(jnp.int32, (L, L), 1)
    mask = (col <= row) if lower else (col >= row)
    logw = jnp.where(mask, gcol + brow, -jnp.inf)
    log_prev = gcol + m_prev
    m_new = jnp.maximum(log_prev, jnp.max(logw, axis=1, keepdims=True))
    w_intra = jnp.exp(logw - m_new)
    w_prev = jnp.exp(log_prev - m_new)

    if out_ref is not None:
        p = _dot(q, kt) * w_intra
        n_row = n_ref[slot]
        qn = jnp.sum(q.astype(F32) * n_row, axis=1, keepdims=True)
        den = jnp.sum(p, axis=1, keepdims=True) + w_prev * qn
        num = _dot(p.astype(BF16), v) + w_prev * _dot(q, ct_ref[slot].astype(BF16))
        inv = 1.0 / jnp.maximum(jnp.abs(den), jnp.exp(-m_new))
        out_ref[0, :, hs] = (num * inv).astype(out_ref.dtype)

    rowv = lax.broadcasted_iota(jnp.int32, (L, 1), 0)
    last = (rowv == (L - 1 if lower else 0))
    g_last = jnp.sum(jnp.where(last, gcol, 0.0), axis=0, keepdims=True)
    m_end = jnp.sum(jnp.where(last, m_new, 0.0), axis=0, keepdims=True)
    w_src = jnp.exp(bcol + g_last - m_end)
    decay = jnp.exp(g_last + m_prev - m_end)
    vw = (w_src * v.astype(F32)).astype(BF16)
    ct_ref[slot] = decay * ct_ref[slot] + _dot(kt, vw)
    n_ref[slot] = decay * n_ref[slot] + jnp.sum(w_src * k_ref[0, :, hs].astype(F32), axis=0, keepdims=True)
    m_ref[slot] = jnp.broadcast_to(m_end, m_ref.shape[1:])


def _scan_kernel(*refs, L, heads, dh, emit_ctx):
    ctx_in = refs[0:7]
    fwd_in = refs[7:14]
    rev_in = refs[14:21]
    hf_ref, hr_ref = refs[21:23]
    if emit_ctx:
        hcf_ref, hcr_ref = refs[23:25]
        ct_ref, n_ref, m_ref = refs[25:28]
    else:
        hcf_ref = hcr_ref = None
        ct_ref, n_ref, m_ref = refs[23:26]
    s = pl.program_id(1)
    step = functools.partial(_chain_step, ct_ref=ct_ref, n_ref=n_ref, m_ref=m_ref, L=L, dh=dh)

    @pl.when(s == 0)
    def _():
        ct_ref[...] = jnp.zeros_like(ct_ref)
        n_ref[...] = jnp.zeros_like(n_ref)
        m_ref[...] = jnp.zeros_like(m_ref)
        for h in range(heads):
            step(*ctx_in, hcf_ref, slot=h, head=h, lane_idx=h, lower=True)
            step(*ctx_in, hcr_ref, slot=heads + h, head=h, lane_idx=heads + h, lower=False)

    @pl.when(s > 0)
    def _():
        for h in range(heads):
            step(*fwd_in, hf_ref, slot=h, head=h, lane_idx=h, lower=True)
            step(*rev_in, hr_ref, slot=heads + h, head=h, lane_idx=heads + h, lower=False)


def _scan(lat, ctx, *, heads, emit_ctx):
    b, n, e = lat["q"].shape
    nc = ctx["q"].shape[1]
    L = SCAN_CHUNK
    assert nc == L, "the context prefix is processed as exactly one chunk"
    dh = e // heads
    n_lat = n // L

    def specs(imap):
        tok = lambda bi, s: (bi, imap(s), 0)
        return [pl.BlockSpec((1, L, e), tok),
                pl.BlockSpec((1, L, e), tok),
                pl.BlockSpec((1, 1, e, L), lambda bi, s: (bi, imap(s), 0, 0)),
                pl.BlockSpec((1, L, e), tok),
                pl.BlockSpec((1, L, GATE_LANES), tok),
                pl.BlockSpec((1, L, GATE_LANES), tok),
                pl.BlockSpec((1, 8, L), lambda bi, s: (bi, 0, imap(s)))]

    def args(d):
        return [d["q"], d["k"], d["kt"], d["a"], d["gcol"], d["bcol"], d["brow"]]

    ctx_map = lambda s: 0
    fwd_map = lambda s: jnp.maximum(s - 1, 0)
    rev_map = lambda s: n_lat - jnp.maximum(s, 1)
    out_specs = [pl.BlockSpec((1, L, e), lambda bi, s: (bi, fwd_map(s), 0)),
                 pl.BlockSpec((1, L, e), lambda bi, s: (bi, rev_map(s), 0))]
    out_shape = [jax.ShapeDtypeStruct((b, n, e), BF16), jax.ShapeDtypeStruct((b, n, e), BF16)]
    if emit_ctx:
        out_specs += [pl.BlockSpec((1, L, e), lambda bi, s: (bi, 0, 0))] * 2
        out_shape += [jax.ShapeDtypeStruct((b, nc, e), BF16)] * 2
    return pl.pallas_call(
        functools.partial(_scan_kernel, L=L, heads=heads, dh=dh, emit_ctx=emit_ctx),
        grid=(b, n_lat + 1),
        in_specs=specs(ctx_map) + specs(fwd_map) + specs(rev_map),
        out_specs=out_specs,
        out_shape=out_shape,
        scratch_shapes=[pltpu.VMEM((2 * heads, dh, dh), F32),
                        pltpu.VMEM((2 * heads, 1, dh), F32),
                        pltpu.VMEM((2 * heads, 1, GATE_LANES), F32)],
        compiler_params=_cparams("parallel", "arbitrary"),
        name="mlstm_scan",
    )(*(args(ctx) + args(lat) + args(lat)))


def _mlstm_out_kernel(hf_ref, hr_ref, o_ref, z_ref, xc_ref, x_ref, lnw_ref, skip_ref, w_ref, mod_ref, out_ref,
                      *, d, dh, heads):
    hs = hf_ref[0].astype(F32) + hr_ref[0].astype(F32)
    parts = []
    for hh in range(heads):
        seg = hs[:, hh * dh:(hh + 1) * dh]
        cen = seg - jnp.mean(seg, axis=-1, keepdims=True)
        var = jnp.mean(cen * cen, axis=-1, keepdims=True)
        parts.append(cen * lax.rsqrt(var + EPS))
    hn = jnp.concatenate(parts, axis=1) * lnw_ref[...]
    hn = _sigmoid(o_ref[0].astype(F32)) * hn
    y = (hn + skip_ref[...] * xc_ref[0].astype(F32)) * _silu(z_ref[0].astype(F32))
    yo = _dot(y.astype(BF16), w_ref[...])
    out_ref[0] = x_ref[0] + mod_ref[0][:, 2 * d:3 * d] * yo


def _mlstm_out(hf, hr, a, xconv, x, lnw, skip, w, mod, *, tm, heads):
    b, n, d = x.shape
    e = hf.shape[2]
    per_batch = mod.shape[0] != 1
    mod_map = (lambda bi, t: (bi, 0, 0)) if per_batch else (lambda bi, t: (0, 0, 0))
    tok = lambda bi, t: (bi, t, 0)
    vec = pl.BlockSpec((1, e), lambda bi, t: (0, 0))
    return pl.pallas_call(
        functools.partial(_mlstm_out_kernel, d=d, dh=e // heads, heads=heads),
        grid=(b, n // tm),
        in_specs=[pl.BlockSpec((1, tm, e), tok),
                  pl.BlockSpec((1, tm, e), tok),
                  pl.BlockSpec((1, tm, e), lambda bi, t: (bi, t, 1)),
                  pl.BlockSpec((1, tm, e), lambda bi, t: (bi, t, 2)),
                  pl.BlockSpec((1, tm, e), tok),
                  pl.BlockSpec((1, tm, d), tok),
                  vec, vec,
                  pl.BlockSpec((e, d), lambda bi, t: (0, 0)),
                  pl.BlockSpec((1, 1, 3 * d), mod_map)],
        out_specs=pl.BlockSpec((1, tm, d), tok),
        out_shape=jax.ShapeDtypeStruct((b, n, d), F32),
        compiler_params=_cparams("parallel", "parallel"),
        name="mlstm_out",
    )(hf, hr, a, a, xconv, x, lnw, skip, w, mod)


def _chan_dft_kernel(u_ref, w_ref, a_ref, b_ref, *, groups, gd):
    for g in range(groups):
        r = _dot(u_ref[0, :, g * gd:(g + 1) * gd], w_ref[...])
        a_ref[0, :, g * gd:(g + 1) * gd] = r[:, :gd].astype(BF16)
        b_ref[0, :, g * gd:(g + 1) * gd] = r[:, gd:].astype(BF16)


def _chan_dft(uz, wc, *, tm, groups):
    b, n, e2 = uz.shape
    e = e2 // 2
    gd = e // groups
    tok = lambda bi, t: (bi, t, 0)
    return pl.pallas_call(
        functools.partial(_chan_dft_kernel, groups=groups, gd=gd),
        grid=(b, n // tm),
        in_specs=[pl.BlockSpec((1, tm, e), tok), pl.BlockSpec((gd, 2 * gd), lambda bi, t: (0, 0))],
        out_specs=[pl.BlockSpec((1, tm, e), tok)] * 2,
        out_shape=[jax.ShapeDtypeStruct((b, n, e), BF16)] * 2,
        compiler_params=_cparams("parallel", "parallel"),
        name="chan_dft",
    )(uz, wc)


def _pos_dft_kernel(cn_ref, sn_ref, a_ref, b_ref, y_ref):
    y_ref[0] = (_dot(cn_ref[...], a_ref[0]) - _dot(sn_ref[...], b_ref[0])).astype(y_ref.dtype)


def _pos_dft(cn, sn, a, bm, *, tm, tn):
    b, n, e = a.shape
    return pl.pallas_call(
        _pos_dft_kernel,
        grid=(b, n // tm, e // tn),
        in_specs=[pl.BlockSpec((tm, n), lambda bi, i, j: (i, 0)),
                  pl.BlockSpec((tm, n), lambda bi, i, j: (i, 0)),
                  pl.BlockSpec((1, n, tn), lambda bi, i, j: (bi, 0, j)),
                  pl.BlockSpec((1, n, tn), lambda bi, i, j: (bi, 0, j))],
        out_specs=pl.BlockSpec((1, tm, tn), lambda bi, i, j: (bi, i, j)),
        out_shape=jax.ShapeDtypeStruct((b, n, e), BF16),
        compiler_params=_cparams("parallel", "parallel", "parallel"),
        name="pos_dft",
    )(cn, sn, a, bm)


def _small_dft(zs):
    n = len(zs)
    if n == 1:
        return zs
    even = _small_dft(zs[0::2])
    odd = _small_dft(zs[1::2])
    out = [None] * n
    for k in range(n // 2):
        o_re, o_im = odd[k]
        if k == 0:
            t_re, t_im = o_re, o_im
        elif 4 * k == n:
            t_re, t_im = o_im, -o_re
        else:
            c = float(np.cos(2.0 * np.pi * k / n))
            s = float(np.sin(2.0 * np.pi * k / n))
            t_re, t_im = c * o_re + s * o_im, c * o_im - s * o_re
        e_re, e_im = even[k]
        out[k] = (e_re + t_re, e_im + t_im)
        out[k + n // 2] = (e_re - t_re, e_im - t_im)
    return out


def _fft_a_kernel(u_ref, wc_ref, tc_ref, ts_ref, gr_ref, gi_ref, *, n1, gd):
    zs = []
    for a in range(n1):
        r = _dot(u_ref[0, a], wc_ref[...])
        zs.append((r[:, :gd], r[:, gd:]))
    gs = _small_dft(zs)
    for k1 in range(n1):
        re, im = gs[k1]
        if k1 > 0:
            c = tc_ref[k1]
            s = ts_ref[k1]
            re, im = re * c + im * s, im * c - re * s
        gr_ref[0, k1] = re.astype(BF16)
        gi_ref[0, k1] = im.astype(BF16)


def _fft_a(uz, wc, tc, ts, *, n1, groups, tr):
    b, n, e2 = uz.shape
    e = e2 // 2
    gd = e // groups
    n2 = n // n1
    blk = pl.BlockSpec((1, n1, tr, gd), lambda bi, r, g: (bi, 0, r, g))
    tw = pl.BlockSpec((n1, tr, gd), lambda bi, r, g: (0, r, 0))
    return pl.pallas_call(
        functools.partial(_fft_a_kernel, n1=n1, gd=gd),
        grid=(b, n2 // tr, groups),
        in_specs=[blk, pl.BlockSpec((gd, 2 * gd), lambda bi, r, g: (0, 0)), tw, tw],
        out_specs=[blk, blk],
        out_shape=[jax.ShapeDtypeStruct((b, n1, n2, e), BF16)] * 2,
        compiler_params=_cparams("parallel", "parallel", "parallel"),
        name="fft_a",
    )(uz.reshape(b, n1, n2, e2), wc, tc, ts)


def _fft_b_kernel(wc_ref, ws_ref, gr_ref, gi_ref, y_ref, scr_ref, *, n1, n2):
    for k1 in range(n1):
        yk = _dot(wc_ref[...], gr_ref[0, k1]) + _dot(ws_ref[...], gi_ref[0, k1])
        for c in range(scr_ref.shape[0]):
            scr_ref[c, pl.ds(k1, n2, stride=n1), :] = yk[:, c * 128:(c + 1) * 128]
    for c in range(scr_ref.shape[0]):
        y_ref[0, :, c * 128:(c + 1) * 128] = scr_ref[c].astype(y_ref.dtype)


def _fft_b(w2c, w2s, gr, gi, *, tn):
    b, n1, n2, e = gr.shape
    blk = pl.BlockSpec((1, n1, n2, tn), lambda bi, j: (bi, 0, 0, j))
    wsp = pl.BlockSpec((n2, n2), lambda bi, j: (0, 0))
    return pl.pallas_call(
        functools.partial(_fft_b_kernel, n1=n1, n2=n2),
        grid=(b, e // tn),
        in_specs=[wsp, wsp, blk, blk],
        out_specs=pl.BlockSpec((1, n1 * n2, tn), lambda bi, j: (bi, 0, j)),
        out_shape=jax.ShapeDtypeStruct((b, n1 * n2, e), BF16),
        scratch_shapes=[pltpu.VMEM((tn // 128, n1 * n2, 128), F32)],
        compiler_params=_cparams("parallel", "parallel"),
        name="fft_b",
    )(w2c, w2s, gr, gi)


def _fnet_out_kernel(y_ref, z_ref, x_ref, w_ref, mod_ref, nf_ref, out_ref, *, d, final):
    y = y_ref[0].astype(F32) * _silu(z_ref[0].astype(F32))
    yo = _dot(y.astype(BF16), w_ref[...])
    xn = x_ref[0] + mod_ref[0][:, 2 * d:3 * d] * yo
    if final:
        xn = xn * lax.rsqrt(jnp.mean(xn * xn, axis=-1, keepdims=True) + EPS) * nf_ref[...]
    out_ref[0] = xn


def _fnet_out(y, uz, x, w, mod, nf, *, tm, final):
    b, n, d = x.shape
    e = y.shape[2]
    per_batch = mod.shape[0] != 1
    mod_map = (lambda bi, t: (bi, 0, 0)) if per_batch else (lambda bi, t: (0, 0, 0))
    tok = lambda bi, t: (bi, t, 0)
    return pl.pallas_call(
        functools.partial(_fnet_out_kernel, d=d, final=final),
        grid=(b, n // tm),
        in_specs=[pl.BlockSpec((1, tm, e), tok),
                  pl.BlockSpec((1, tm, e), lambda bi, t: (bi, t, 1)),
                  pl.BlockSpec((1, tm, d), tok),
                  pl.BlockSpec((e, d), lambda bi, t: (0, 0)),
                  pl.BlockSpec((1, 1, 3 * d), mod_map),
                  pl.BlockSpec((1, d), lambda bi, t: (0, 0))],
        out_specs=pl.BlockSpec((1, tm, d), tok),
        out_shape=jax.ShapeDtypeStruct((b, n, d), F32),
        compiler_params=_cparams("parallel", "parallel"),
        name="fnet_out",
    )(y, uz, x, w, mod, nf)


def _dft_mats(n, norm):
    k = lax.broadcasted_iota(jnp.int32, (n, n), 0)
    m = lax.broadcasted_iota(jnp.int32, (n, n), 1)
    ang = ((k * m) % n).astype(F32) * (2.0 * np.pi / n)
    return (jnp.cos(ang) * norm).astype(BF16), (jnp.sin(ang) * norm).astype(BF16)


def _mlstm_layer(xl, xc, mod_l, mod_c, gain, p, *, update_ctx):
    heads = MLSTM_HEADS
    n = xl.shape[1]
    nc = xc.shape[1]
    w_in = p["w_in"].astype(BF16)
    e = w_in.shape[1] // 3
    dh = e // heads
    cw = p["conv_w"].reshape(9, e)
    cb = p["conv_b"].reshape(1, e)
    wq = p["w_q"].astype(BF16)
    wk = p["w_k"].astype(BF16)
    wif = p["w_if"].reshape(2, 3, heads, dh, 2, heads)
    wg = jnp.transpose(wif, (2, 1, 3, 4, 0, 5)).reshape(heads, 3, dh, 4 * heads)
    wg = jnp.pad(wg, ((0, 0), (0, 0), (0, 0), (0, GATE_LANES - 4 * heads))).astype(BF16)
    bias = jnp.concatenate([p["b_i"].reshape(-1), p["b_f"].reshape(-1)])
    bias = jnp.pad(bias, (0, GATE_LANES - 4 * heads)).reshape(1, GATE_LANES)
    lnw = p["ln_w"].reshape(1, e)
    skip = p["skip"].reshape(1, e)
    w_out = p["w_out"].astype(BF16)

    def branch(x, mod, tm, img_w, has_vert):
        a = _norm_matmul(x, mod, gain, w_in, tm=tm, tn=e)
        xconv, q, k, kt, gates = _conv_qk(a, cw, cb, wq, wk, wg, tm=tm, img_w=img_w, has_vert=has_vert)
        gcol, bcol, brow = _gate_fin(gates, bias, heads=heads)
        return dict(a=a, xconv=xconv, q=q, k=k, kt=kt, gcol=gcol, bcol=bcol, brow=brow)

    lat = branch(xl, mod_l, min(1024, n), GRID_W, True)
    ctx = branch(xc, mod_c, nc, nc, False)
    outs = _scan(lat, ctx, heads=heads, emit_ctx=update_ctx)
    xl = _mlstm_out(outs[0], outs[1], lat["a"], lat["xconv"], xl, lnw, skip, w_out, mod_l,
                    tm=min(256, n), heads=heads)
    if update_ctx:
        xc = _mlstm_out(outs[2], outs[3], ctx["a"], ctx["xconv"], xc, lnw, skip, w_out, mod_c,
                        tm=nc, heads=heads)
    return xl, xc


def _fourier_layer(xl, xc, mod_l, mod_c, gain, p, nf, *, update_ctx, final):
    groups = FOURIER_GROUPS
    n = xl.shape[1]
    nc = xc.shape[1]
    w_in = p["w_in"].astype(BF16)
    w_out = p["w_out"].astype(BF16)
    e = w_in.shape[1] // 2
    gd = e // groups
    cc, sc = _dft_mats(gd, gd ** -0.5)
    wc = jnp.concatenate([cc, sc], axis=1)

    def position_dft(uz, m):
        n1 = FFT_SLABS if m % (FFT_SLABS * 256) == 0 else 1
        if n1 == 1:
            a, bm = _chan_dft(uz, wc, tm=min(1024, m), groups=groups)
            cn, sn = _dft_mats(m, m ** -0.5)
            return _pos_dft(cn, sn, a, bm, tm=min(512, m), tn=min(512, e))
        n2 = m // n1
        k1 = lax.broadcasted_iota(jnp.int32, (n1, n2, gd), 0)
        p2 = lax.broadcasted_iota(jnp.int32, (n1, n2, gd), 1)
        ang = ((k1 * p2) % m).astype(F32) * (2.0 * np.pi / m)
        gr, gi = _fft_a(uz, jnp.concatenate([cc, -sc], axis=1), jnp.cos(ang), jnp.sin(ang),
                        n1=n1, groups=groups, tr=min(512, n2))
        w2c, w2s = _dft_mats(n2, m ** -0.5)
        return _fft_b(w2c, w2s, gr, gi, tn=256)

    def branch(x, mod, tm, fin):
        m = x.shape[1]
        uz = _norm_matmul(x, mod, gain, w_in, tm=tm, tn=e)
        y = position_dft(uz, m)
        return _fnet_out(y, uz, x, w_out, mod, nf, tm=min(512, m), final=fin)

    xl = branch(xl, mod_l, min(1024, n), final)
    if update_ctx:
        xc = branch(xc, mod_c, nc, False)
    return xl, xc


def kernel(x, c, ctx, c_ctx, norm_g, w_ada, b_ada, m_w_in, m_conv_w, m_conv_b, m_w_q, m_w_k, m_w_if, m_b_i, m_b_f,
           m_ln_w, m_skip, m_w_out, f_w_in, f_w_out, norm_f):
    b, n, d = x.shape
    depth = norm_g.shape[0]
    assert depth % N_MIXERS == 0, "the final norm is fused into the last Fourier layer"
    last_mlstm = ((depth - 1) // N_MIXERS) * N_MIXERS
    rows = -(-(b + 1) // 8) * 8
    cc = jnp.concatenate([c, c_ctx[None, :], jnp.zeros((rows - b - 1, d), F32)], axis=0)
    mods = _ada(cc, w_ada, b_ada)
    nf = norm_f.reshape(1, d)
    xl, xc = x, ctx
    for i in range(depth):
        j = i // N_MIXERS
        update_ctx = i < last_mlstm
        mod_l = mods[i, :b][:, None, :]
        mod_c = mods[i, b:b + 1][:, None, :]
        gain = norm_g[i].reshape(1, d)
        if i % N_MIXERS == 0:
            p = dict(w_in=m_w_in[j], conv_w=m_conv_w[j], conv_b=m_conv_b[j], w_q=m_w_q[j], w_k=m_w_k[j],
                     w_if=m_w_if[j], b_i=m_b_i[j], b_f=m_b_f[j], ln_w=m_ln_w[j], skip=m_skip[j], w_out=m_w_out[j])
            xl, xc = _mlstm_layer(xl, xc, mod_l, mod_c, gain, p, update_ctx=update_ctx)
        else:
            p = dict(w_in=f_w_in[j], w_out=f_w_out[j])
            xl, xc = _fourier_layer(xl, xc, mod_l, mod_c, gain, p, nf, update_ctx=update_ctx,
                                    final=(i == depth - 1))
    return xl
```

```python
import functools

import numpy as np
import jax
import jax.numpy as jnp
from jax import lax
from jax.experimental import pallas as pl
from jax.experimental.pallas import tpu as pltpu

F32 = jnp.float32
BF16 = jnp.bfloat16

GRID_W = 64
N_MIXERS = 2
MLSTM_HEADS = 4
FOURIER_GROUPS = 8
EPS = 1e-6
SCAN_CHUNK = 256
CONV_SUB = SCAN_CHUNK
GATE_LANES = 128
FFT_SLABS = 8
VMEM_LIMIT = 56 * 1024 * 1024


def _cparams(*sem):
    return pltpu.CompilerParams(dimension_semantics=sem, vmem_limit_bytes=VMEM_LIMIT)


def _sigmoid(v):
    return 1.0 / (1.0 + jnp.exp(-v))


def _silu(v):
    return v * _sigmoid(v)


def _dot(a, b):
    return jnp.dot(a, b, preferred_element_type=F32)


def _ada_kernel(c_ref, w_ref, b_ref, o_ref):
    s = _silu(c_ref[...])
    o_ref[0] = _dot(s.astype(BF16), w_ref[0].astype(BF16)) + b_ref[0]


def _ada(cc, w_ada, b_ada):
    depth, d, d3 = w_ada.shape
    r = cc.shape[0]
    tn = 512
    return pl.pallas_call(
        _ada_kernel,
        grid=(depth, d3 // tn),
        in_specs=[pl.BlockSpec((r, d), lambda i, j: (0, 0)),
                  pl.BlockSpec((1, d, tn), lambda i, j: (i, 0, j)),
                  pl.BlockSpec((1, 1, tn), lambda i, j: (i, 0, j))],
        out_specs=pl.BlockSpec((1, r, tn), lambda i, j: (i, 0, j)),
        out_shape=jax.ShapeDtypeStruct((depth, r, d3), F32),
        compiler_params=_cparams("parallel", "parallel"),
        name="ada",
    )(cc, w_ada, b_ada.reshape(depth, 1, d3))


def _norm_matmul_kernel(x_ref, mod_ref, g_ref, w_ref, o_ref, *, d):
    x = x_ref[0]
    y = x * lax.rsqrt(jnp.mean(x * x, axis=-1, keepdims=True) + EPS) * g_ref[...]
    mod = mod_ref[0]
    h = y * (1.0 + mod[:, d:2 * d]) + mod[:, :d]
    o_ref[0] = _dot(h.astype(BF16), w_ref[...]).astype(o_ref.dtype)


def _norm_matmul(x, mod, gain, w, *, tm, tn):
    b, n, d = x.shape
    no = w.shape[1]
    per_batch = mod.shape[0] != 1
    mod_map = (lambda j, bi, t: (bi, 0, 0)) if per_batch else (lambda j, bi, t: (0, 0, 0))
    return pl.pallas_call(
        functools.partial(_norm_matmul_kernel, d=d),
        grid=(no // tn, b, n // tm),
        in_specs=[pl.BlockSpec((1, tm, d), lambda j, bi, t: (bi, t, 0)),
                  pl.BlockSpec((1, 1, 3 * d), mod_map),
                  pl.BlockSpec((1, d), lambda j, bi, t: (0, 0)),
                  pl.BlockSpec((d, tn), lambda j, bi, t: (0, j))],
        out_specs=pl.BlockSpec((1, tm, tn), lambda j, bi, t: (bi, t, j)),
        out_shape=jax.ShapeDtypeStruct((b, n, no), BF16),
        compiler_params=_cparams("parallel", "parallel", "parallel"),
        name="norm_matmul",
    )(x, mod, gain, w)


def _conv_qk_kernel(xc_ref, xp_ref, xn_ref, cw_ref, cb_ref, wq_ref, wk_ref, wg_ref,
                    xconv_ref, q_ref, k_ref, kt_ref, gates_ref, *, tm, img_w, has_vert, n_tiles, scale):
    t = pl.program_id(1)
    sub = CONV_SUB
    n_sub = tm // sub
    for c in range(n_sub):
        lo = c * sub
        xcb = xc_ref[0, lo:lo + sub, :]
        xc = xcb.astype(F32)
        if has_vert:
            if c == 0:
                top = jnp.where(t > 0, xp_ref[0].astype(F32), 0.0)
            else:
                top = xc_ref[0, lo - img_w:lo, :].astype(F32)
            if c == n_sub - 1:
                bot = jnp.where(t < n_tiles - 1, xn_ref[0].astype(F32), 0.0)
            else:
                bot = xc_ref[0, lo + sub:lo + sub + img_w, :].astype(F32)
            ext = jnp.concatenate([top, xc, bot], axis=0)
            off = img_w
            taps = (-1, 0, 1)
        else:
            ext = xc
            off = 0
            taps = (0,)
        rows = ext.shape[0]
        col = lax.broadcasted_iota(jnp.int32, (rows, 1), 0) % img_w
        left = jnp.where(col == 0, 0.0, pltpu.roll(ext, 1, 0))
        right = jnp.where(col == img_w - 1, 0.0, pltpu.roll(ext, rows - 1, 0))
        acc = None
        for dr in taps:
            s = off + dr * img_w
            r = 3 * (dr + 1)
            term = (cw_ref[r:r + 1, :] * left[s:s + sub] + cw_ref[r + 1:r + 2, :] * ext[s:s + sub]
                    + cw_ref[r + 2:r + 3, :] * right[s:s + sub])
            acc = term if acc is None else acc + term
        xv = _silu(acc + cb_ref[...]).astype(BF16)
        xconv_ref[0, lo:lo + sub, :] = xv
        qb = _dot(xv, wq_ref[0]).astype(BF16)
        kf = _dot(xv, wk_ref[0]) * scale
        kb = kf.astype(BF16)
        q_ref[0, lo:lo + sub, :] = qb
        k_ref[0, lo:lo + sub, :] = kb
        kt_ref[0, c] = jnp.transpose(kf).astype(BF16)
        gates_ref[0, 0, lo:lo + sub, :] = (_dot(qb, wg_ref[0, 0]) + _dot(kb, wg_ref[0, 1])
                                           + _dot(xcb, wg_ref[0, 2]))


def _conv_qk(a, cw, cb, wq, wk, wg, *, tm, img_w, has_vert):
    b, n, _ = a.shape
    heads, dh, _ = wq.shape
    e = heads * dh
    n_tiles = n // tm
    sub = CONV_SUB
    assert tm % sub == 0
    hb = img_w if has_vert else 16
    per = tm // hb
    last = n // hb - 1
    kern = functools.partial(_conv_qk_kernel, tm=tm, img_w=img_w, has_vert=has_vert, n_tiles=n_tiles,
                             scale=float(dh) ** -0.5)
    tok = lambda bi, t, h: (bi, t, h)
    return pl.pallas_call(
        kern,
        grid=(b, n_tiles, heads),
        in_specs=[pl.BlockSpec((1, tm, dh), tok),
                  pl.BlockSpec((1, hb, dh), lambda bi, t, h: (bi, jnp.maximum(t * per - 1, 0), h)),
                  pl.BlockSpec((1, hb, dh), lambda bi, t, h: (bi, jnp.minimum((t + 1) * per, last), h)),
                  pl.BlockSpec((9, dh), lambda bi, t, h: (0, h)),
                  pl.BlockSpec((1, dh), lambda bi, t, h: (0, h)),
                  pl.BlockSpec((1, dh, dh), lambda bi, t, h: (h, 0, 0)),
                  pl.BlockSpec((1, dh, dh), lambda bi, t, h: (h, 0, 0)),
                  pl.BlockSpec((1, 3, dh, GATE_LANES), lambda bi, t, h: (h, 0, 0, 0))],
        out_specs=[pl.BlockSpec((1, tm, dh), tok),
                   pl.BlockSpec((1, tm, dh), tok),
                   pl.BlockSpec((1, tm, dh), tok),
                   pl.BlockSpec((1, tm // sub, dh, sub), lambda bi, t, h: (bi, t, h, 0)),
                   pl.BlockSpec((1, 1, tm, GATE_LANES), lambda bi, t, h: (bi, h, t, 0))],
        out_shape=[jax.ShapeDtypeStruct((b, n, e), BF16),
                   jax.ShapeDtypeStruct((b, n, e), BF16),
                   jax.ShapeDtypeStruct((b, n, e), BF16),
                   jax.ShapeDtypeStruct((b, n // sub, e, sub), BF16),
                   jax.ShapeDtypeStruct((b, heads, n, GATE_LANES), F32)],
        compiler_params=_cparams("parallel", "parallel", "parallel"),
        name="conv_qk",
    )(a, a, a, cw, cb, wq, wk, wg)


def _split3(a):
    hi = a.astype(BF16)
    r1 = a - hi.astype(F32)
    mid = r1.astype(BF16)
    lo = (r1 - mid.astype(F32)).astype(BF16)
    return hi, mid, lo


def _gate_fin_kernel(g_ref, bias_ref, gcol_ref, bcol_ref, brow_ref, *, L, heads):
    nd = 2 * heads
    row = lax.broadcasted_iota(jnp.int32, (L, L), 0)
    col = lax.broadcasted_iota(jnp.int32, (L, L), 1)
    tri_l = (col <= row).astype(BF16)
    tri_u = (col >= row).astype(BF16)
    lane = lax.broadcasted_iota(jnp.int32, (L, GATE_LANES), 1)
    for c0 in range(0, g_ref.shape[2], L):
        rows = slice(c0, c0 + L)
        pre = jnp.sum(g_ref[0, :, rows, :], axis=0) + bias_ref[...]
        lf = jnp.minimum(pre, 0.0) - jnp.log(1.0 + jnp.exp(-jnp.abs(pre)))
        parts = _split3(lf)
        g_l = _dot(tri_l, parts[0]) + _dot(tri_l, parts[1]) + _dot(tri_l, parts[2])
        g_u = _dot(tri_u, parts[0]) + _dot(tri_u, parts[1]) + _dot(tri_u, parts[2])
        g_sel = jnp.where(lane < nd + heads, g_l, g_u)
        g = pltpu.roll(g_sel, GATE_LANES - nd, 1)
        bc = pre - g
        gcol_ref[0, rows, :] = g
        bcol_ref[0, rows, :] = bc
        brow_ref[0, :, rows] = jnp.transpose(bc)[0:8, :]


def _gate_fin(gates, bias, *, heads):
    b, _, n, _ = gates.shape
    L = SCAN_CHUNK
    assert 2 * heads == 8
    tr = min(n, 4 * L)
    blk = pl.BlockSpec((1, tr, GATE_LANES), lambda bi, c: (bi, c, 0))
    return pl.pallas_call(
        functools.partial(_gate_fin_kernel, L=L, heads=heads),
        grid=(b, n // tr),
        in_specs=[pl.BlockSpec((1, heads, tr, GATE_LANES), lambda bi, c: (bi, 0, c, 0)),
                  pl.BlockSpec((1, GATE_LANES), lambda bi, c: (0, 0))],
        out_specs=[blk, blk, pl.BlockSpec((1, 8, tr), lambda bi, c: (bi, 0, c))],
        out_shape=[jax.ShapeDtypeStruct((b, n, GATE_LANES), F32),
                   jax.ShapeDtypeStruct((b, n, GATE_LANES), F32),
                   jax.ShapeDtypeStruct((b, 8, n), F32)],
        compiler_params=_cparams("parallel", "parallel"),
        name="gate_fin",
    )(gates, bias)


def _chain_step(q_ref, k_ref, kt_ref, v_ref, gc_ref, bc_ref, br_ref, out_ref, ct_ref, n_ref, m_ref,
                *, slot, head, lane_idx, lower, L, dh):
    hs = slice(head * dh, (head + 1) * dh)
    q = q_ref[0, :, hs]
    kt = kt_ref[0, 0, hs, :]
    v = v_ref[0, :, hs]
    gcol = gc_ref[0, :, lane_idx:lane_idx + 1]
    bcol = bc_ref[0, :, lane_idx:lane_idx + 1]
    brow = br_ref[0, lane_idx:lane_idx + 1, :]
    m_prev = m_ref[slot][:, 0:1]

    row = lax.broadcasted_iota(jnp.int32, (L, L), 0)
    col = lax.broadcasted_iota(jnp.int32, (L, L), 1)
    mask = (col <= row) if lower else (col >= row)
    logw = jnp.where(mask, gcol + brow, -jnp.inf)
    log_prev = gcol + m_prev
    m_new = jnp.maximum(log_prev, jnp.max(logw, axis=1, keepdims=True))
    w_intra = jnp.exp(logw - m_new)
    w_prev = jnp.exp(log_prev - m_new)

    if out_ref is not None:
        p = _dot(q, kt) * w_intra
        n_row = n_ref[slot]
        qn = jnp.sum(q.astype(F32) * n_row, axis=1, keepdims=True)
        den = jnp.sum(p, axis=1, keepdims=True) + w_prev * qn
        num = _dot(p.astype(BF16), v) + w_prev * _dot(q, ct_ref[slot].astype(BF16))
        inv = 1.0 / jnp.maximum(jnp.abs(den), jnp.exp(-m_new))
        out_ref[0, :, hs] = (num * inv).astype(out_ref.dtype)

    rowv = lax.broadcasted_iota(jnp.int32, (L, 1), 0)
    last = (rowv == (L - 1 if lower else 0))
    g_last = jnp.sum(jnp.where(last, gcol, 0.0), axis=0, keepdims=True)
    m_end = jnp.sum(jnp.where(last, m_new, 0.0), axis=0, keepdims=True)
    w_src = jnp.exp(bcol + g_last - m_end)
    decay = jnp.exp(g_last + m_prev - m_end)
    vw = (w_src * v.astype(F32)).astype(BF16)
    ct_ref[slot] = decay * ct_ref[slot] + _dot(kt, vw)
    n_ref[slot] = decay * n_ref[slot] + jnp.sum(w_src * k_ref[0, :, hs].astype(F32), axis=0, keepdims=True)
    m_ref[slot] = jnp.broadcast_to(m_end, m_ref.shape[1:])


def _scan_kernel(*refs, L, heads, dh, emit_ctx):
    ctx_in = refs[0:7]
    fwd_in = refs[7:14]
    rev_in = refs[14:21]
    hf_ref, hr_ref = refs[21:23]
    if emit_ctx:
        hcf_ref, hcr_ref = refs[23:25]
        ct_ref, n_ref, m_ref = refs[25:28]
    else:
        hcf_ref = hcr_ref = None
        ct_ref, n_ref, m_ref = refs[23:26]
    s = pl.program_id(1)
    step = functools.partial(_chain_step, ct_ref=ct_ref, n_ref=n_ref, m_ref=m_ref, L=L, dh=dh)

    @pl.when(s == 0)
    def _():
        ct_ref[...] = jnp.zeros_like(ct_ref)
        n_ref[...] = jnp.zeros_like(n_ref)
        m_ref[...] = jnp.zeros_like(m_ref)
        for h in range(heads):
            step(*ctx_in, hcf_ref, slot=h, head=h, lane_idx=h, lower=True)
            step(*ctx_in, hcr_ref, slot=heads + h, head=h, lane_idx=heads + h, lower=False)

    @pl.when(s > 0)
    def _():
        for h in range(heads):
            step(*fwd_in, hf_ref, slot=h, head=h, lane_idx=h, lower=True)
            step(*rev_in, hr_ref, slot=heads + h, head=h, lane_idx=heads + h, lower=False)


def _scan(lat, ctx, *, heads, emit_ctx):
    b, n, e = lat["q"].shape
    nc = ctx["q"].shape[1]
    L = SCAN_CHUNK
    assert nc == L, "the context prefix is processed as exactly one chunk"
    dh = e // heads
    n_lat = n // L

    def specs(imap):
        tok = lambda bi, s: (bi, imap(s), 0)
        return [pl.BlockSpec((1, L, e), tok),
                pl.BlockSpec((1, L, e), tok),
                pl.BlockSpec((1, 1, e, L), lambda bi, s: (bi, imap(s), 0, 0)),
                pl.BlockSpec((1, L, e), tok),
                pl.BlockSpec((1, L, GATE_LANES), tok),
                pl.BlockSpec((1, L, GATE_LANES), tok),
                pl.BlockSpec((1, 8, L), lambda bi, s: (bi, 0, imap(s)))]

    def args(d):
        return [d["q"], d["k"], d["kt"], d["a"], d["gcol"], d["bcol"], d["brow"]]

    ctx_map = lambda s: 0
    fwd_map = lambda s: jnp.maximum(s - 1, 0)
    rev_map = lambda s: n_lat - jnp.maximum(s, 1)
    out_specs = [pl.BlockSpec((1, L, e), lambda bi, s: (bi, fwd_map(s), 0)),
                 pl.BlockSpec((1, L, e), lambda bi, s: (bi, rev_map(s), 0))]
    out_shape = [jax.ShapeDtypeStruct((b, n, e), BF16), jax.ShapeDtypeStruct((b, n, e), BF16)]
    if emit_ctx:
        out_specs += [pl.BlockSpec((1, L, e), lambda bi, s: (bi, 0, 0))] * 2
        out_shape += [jax.ShapeDtypeStruct((b, nc, e), BF16)] * 2
    return pl.pallas_call(
        functools.partial(_scan_kernel, L=L, heads=heads, dh=dh, emit_ctx=emit_ctx),
        grid=(b, n_lat + 1),
        in_specs=specs(ctx_map) + specs(fwd_map) + specs(rev_map),
        out_specs=out_specs,
        out_shape=out_shape,
        scratch_shapes=[pltpu.VMEM((2 * heads, dh, dh), F32),
                        pltpu.VMEM((2 * heads, 1, dh), F32),
                        pltpu.VMEM((2 * heads, 1, GATE_LANES), F32)],
        compiler_params=_cparams("parallel", "arbitrary"),
        name="mlstm_scan",
    )(*(args(ctx) + args(lat) + args(lat)))


def _mlstm_out_kernel(hf_ref, hr_ref, o_ref, z_ref, xc_ref, x_ref, lnw_ref, skip_ref, w_ref, mod_ref, out_ref,
                      *, d, dh, heads):
    hs = hf_ref[0].astype(F32) + hr_ref[0].astype(F32)
    parts = []
    for hh in range(heads):
        seg = hs[:, hh * dh:(hh + 1) * dh]
        cen = seg - jnp.mean(seg, axis=-1, keepdims=True)
        var = jnp.mean(cen * cen, axis=-1, keepdims=True)
        parts.append(cen * lax.rsqrt(var + EPS))
    hn = jnp.concatenate(parts, axis=1) * lnw_ref[...]
    hn = _sigmoid(o_ref[0].astype(F32)) * hn
    y = (hn + skip_ref[...] * xc_ref[0].astype(F32)) * _silu(z_ref[0].astype(F32))
    yo = _dot(y.astype(BF16), w_ref[...])
    out_ref[0] = x_ref[0] + mod_ref[0][:, 2 * d:3 * d] * yo


def _mlstm_out(hf, hr, a, xconv, x, lnw, skip, w, mod, *, tm, heads):
    b, n, d = x.shape
    e = hf.shape[2]
    per_batch = mod.shape[0] != 1
    mod_map = (lambda bi, t: (bi, 0, 0)) if per_batch else (lambda bi, t: (0, 0, 0))
    tok = lambda bi, t: (bi, t, 0)
    vec = pl.BlockSpec((1, e), lambda bi, t: (0, 0))
    return pl.pallas_call(
        functools.partial(_mlstm_out_kernel, d=d, dh=e // heads, heads=heads),
        grid=(b, n // tm),
        in_specs=[pl.BlockSpec((1, tm, e), tok),
                  pl.BlockSpec((1, tm, e), tok),
                  pl.BlockSpec((1, tm, e), lambda bi, t: (bi, t, 1)),
                  pl.BlockSpec((1, tm, e), lambda bi, t: (bi, t, 2)),
                  pl.BlockSpec((1, tm, e), tok),
                  pl.BlockSpec((1, tm, d), tok),
                  vec, vec,
                  pl.BlockSpec((e, d), lambda bi, t: (0, 0)),
                  pl.BlockSpec((1, 1, 3 * d), mod_map)],
        out_specs=pl.BlockSpec((1, tm, d), tok),
        out_shape=jax.ShapeDtypeStruct((b, n, d), F32),
        compiler_params=_cparams("parallel", "parallel"),
        name="mlstm_out",
    )(hf, hr, a, a, xconv, x, lnw, skip, w, mod)


def _chan_dft_kernel(u_ref, w_ref, a_ref, b_ref, *, groups, gd):
    for g in range(groups):
        r = _dot(u_ref[0, :, g * gd:(g + 1) * gd], w_ref[...])
        a_ref[0, :, g * gd:(g + 1) * gd] = r[:, :gd].astype(BF16)
        b_ref[0, :, g * gd:(g + 1) * gd] = r[:, gd:].astype(BF16)


def _chan_dft(uz, wc, *, tm, groups):
    b, n, e2 = uz.shape
    e = e2 // 2
    gd = e // groups
    tok = lambda bi, t: (bi, t, 0)
    return pl.pallas_call(
        functools.partial(_chan_dft_kernel, groups=groups, gd=gd),
        grid=(b, n // tm),
        in_specs=[pl.BlockSpec((1, tm, e), tok), pl.BlockSpec((gd, 2 * gd), lambda bi, t: (0, 0))],
        out_specs=[pl.BlockSpec((1, tm, e), tok)] * 2,
        out_shape=[jax.ShapeDtypeStruct((b, n, e), BF16)] * 2,
        compiler_params=_cparams("parallel", "parallel"),
        name="chan_dft",
    )(uz, wc)


def _pos_dft_kernel(cn_ref, sn_ref, a_ref, b_ref, y_ref):
    y_ref[0] = (_dot(cn_ref[...], a_ref[0]) - _dot(sn_ref[...], b_ref[0])).astype(y_ref.dtype)


def _pos_dft(cn, sn, a, bm, *, tm, tn):
    b, n, e = a.shape
    return pl.pallas_call(
        _pos_dft_kernel,
        grid=(b, n // tm, e // tn),
        in_specs=[pl.BlockSpec((tm, n), lambda bi, i, j: (i, 0)),
                  pl.BlockSpec((tm, n), lambda bi, i, j: (i, 0)),
                  pl.BlockSpec((1, n, tn), lambda bi, i, j: (bi, 0, j)),
                  pl.BlockSpec((1, n, tn), lambda bi, i, j: (bi, 0, j))],
        out_specs=pl.BlockSpec((1, tm, tn), lambda bi, i, j: (bi, i, j)),
        out_shape=jax.ShapeDtypeStruct((b, n, e), BF16),
        compiler_params=_cparams("parallel", "parallel", "parallel"),
        name="pos_dft",
    )(cn, sn, a, bm)


def _small_dft(zs):
    n = len(zs)
    if n == 1:
        return zs
    even = _small_dft(zs[0::2])
    odd = _small_dft(zs[1::2])
    out = [None] * n
    for k in range(n // 2):
        o_re, o_im = odd[k]
        if k == 0:
            t_re, t_im = o_re, o_im
        elif 4 * k == n:
            t_re, t_im = o_im, -o_re
        else:
            c = float(np.cos(2.0 * np.pi * k / n))
            s = float(np.sin(2.0 * np.pi * k / n))
            t_re, t_im = c * o_re + s * o_im, c * o_im - s * o_re
        e_re, e_im = even[k]
        out[k] = (e_re + t_re, e_im + t_im)
        out[k + n // 2] = (e_re - t_re, e_im - t_im)
    return out


def _fft_a_kernel(u_ref, wc_ref, tc_ref, ts_ref, gr_ref, gi_ref, *, n1, gd):
    zs = []
    for a in range(n1):
        r = _dot(u_ref[0, a], wc_ref[...])
        zs.append((r[:, :gd], r[:, gd:]))
    gs = _small_dft(zs)
    for k1 in range(n1):
        re, im = gs[k1]
        if k1 > 0:
            c = jnp.concatenate([tc_ref[k1]] * (gd // 128), axis=1)
            s = jnp.concatenate([ts_ref[k1]] * (gd // 128), axis=1)
            re, im = re * c + im * s, im * c - re * s
        gr_ref[0, k1] = re.astype(BF16)
        gi_ref[0, k1] = im.astype(BF16)


def _fft_a(uz, wc, tc, ts, *, n1, groups, tr):
    b, n, e2 = uz.shape
    e = e2 // 2
    gd = e // groups
    n2 = n // n1
    blk = pl.BlockSpec((1, n1, tr, gd), lambda bi, r, g: (bi, 0, r, g))
    tw = pl.BlockSpec((n1, tr, 128), lambda bi, r, g: (0, r, 0))
    return pl.pallas_call(
        functools.partial(_fft_a_kernel, n1=n1, gd=gd),
        grid=(b, n2 // tr, groups),
        in_specs=[blk, pl.BlockSpec((gd, 2 * gd), lambda bi, r, g: (0, 0)), tw, tw],
        out_specs=[blk, blk],
        out_shape=[jax.ShapeDtypeStruct((b, n1, n2, e), BF16)] * 2,
        compiler_params=_cparams("parallel", "parallel", "parallel"),
        name="fft_a",
    )(uz.reshape(b, n1, n2, e2), wc, tc, ts)


def _fft_b_kernel(wc_ref, ws_ref, gr_ref, gi_ref, y_ref, scr_ref, *, n1, n2):
    for k1 in range(n1):
        yk = _dot(wc_ref[...], gr_ref[0, k1]) + _dot(ws_ref[...], gi_ref[0, k1])
        for c in range(scr_ref.shape[0]):
            scr_ref[c, pl.ds(k1, n2, stride=n1), :] = yk[:, c * 128:(c + 1) * 128]
    for c in range(scr_ref.shape[0]):
        y_ref[0, :, c * 128:(c + 1) * 128] = scr_ref[c].astype(y_ref.dtype)


def _fft_b(w2c, w2s, gr, gi, *, tn):
    b, n1, n2, e = gr.shape
    blk = pl.BlockSpec((1, n1, n2, tn), lambda bi, j: (bi, 0, 0, j))
    wsp = pl.BlockSpec((n2, n2), lambda bi, j: (0, 0))
    return pl.pallas_call(
        functools.partial(_fft_b_kernel, n1=n1, n2=n2),
        grid=(b, e // tn),
        in_specs=[wsp, wsp, blk, blk],
        out_specs=pl.BlockSpec((1, n1 * n2, tn), lambda bi, j: (bi, 0, j)),
        out_shape=jax.ShapeDtypeStruct((b, n1 * n2, e), BF16),
        scratch_shapes=[pltpu.VMEM((tn // 128, n1 * n2, 128), F32)],
        compiler_params=_cparams("parallel", "parallel"),
        name="fft_b",
    )(w2c, w2s, gr, gi)


def _fnet_out_kernel(y_ref, z_ref, x_ref, w_ref, mod_ref, nf_ref, out_ref, *, d, final):
    y = y_ref[0].astype(F32) * _silu(z_ref[0].astype(F32))
    yo = _dot(y.astype(BF16), w_ref[...])
    xn = x_ref[0] + mod_ref[0][:, 2 * d:3 * d] * yo
    if final:
        xn = xn * lax.rsqrt(jnp.mean(xn * xn, axis=-1, keepdims=True) + EPS) * nf_ref[...]
    out_ref[0] = xn


def _fnet_out(y, uz, x, w, mod, nf, *, tm, final):
    b, n, d = x.shape
    e = y.shape[2]
    per_batch = mod.shape[0] != 1
    mod_map = (lambda bi, t: (bi, 0, 0)) if per_batch else (lambda bi, t: (0, 0, 0))
    tok = lambda bi, t: (bi, t, 0)
    return pl.pallas_call(
        functools.partial(_fnet_out_kernel, d=d, final=final),
        grid=(b, n // tm),
        in_specs=[pl.BlockSpec((1, tm, e), tok),
                  pl.BlockSpec((1, tm, e), lambda bi, t: (bi, t, 1)),
                  pl.BlockSpec((1, tm, d), tok),
                  pl.BlockSpec((e, d), lambda bi, t: (0, 0)),
                  pl.BlockSpec((1, 1, 3 * d), mod_map),
                  pl.BlockSpec((1, d), lambda bi, t: (0, 0))],
        out_specs=pl.BlockSpec((1, tm, d), tok),
        out_shape=jax.ShapeDtypeStruct((b, n, d), F32),
        compiler_params=_cparams("parallel", "parallel"),
        name="fnet_out",
    )(y, uz, x, w, mod, nf)


def _dft_mats(n, norm):
    k = lax.broadcasted_iota(jnp.int32, (n, n), 0)
    m = lax.broadcasted_iota(jnp.int32, (n, n), 1)
    ang = ((k * m) % n).astype(F32) * (2.0 * np.pi / n)
    return (jnp.cos(ang) * norm).astype(BF16), (jnp.sin(ang) * norm).astype(BF16)


def _mlstm_layer(xl, xc, mod_l, mod_c, gain, p, *, update_ctx):
    heads = MLSTM_HEADS
    n = xl.shape[1]
    nc = xc.shape[1]
    w_in = p["w_in"].astype(BF16)
    e = w_in.shape[1] // 3
    dh = e // heads
    cw = p["conv_w"].reshape(9, e)
    cb = p["conv_b"].reshape(1, e)
    wq = p["w_q"].astype(BF16)
    wk = p["w_k"].astype(BF16)
    wif = p["w_if"].reshape(2, 3, heads, dh, 2, heads)
    wg = jnp.transpose(wif, (2, 1, 3, 4, 0, 5)).reshape(heads, 3, dh, 4 * heads)
    wg = jnp.pad(wg, ((0, 0), (0, 0), (0, 0), (0, GATE_LANES - 4 * heads))).astype(BF16)
    bias = jnp.concatenate([p["b_i"].reshape(-1), p["b_f"].reshape(-1)])
    bias = jnp.pad(bias, (0, GATE_LANES - 4 * heads)).reshape(1, GATE_LANES)
    lnw = p["ln_w"].reshape(1, e)
    skip = p["skip"].reshape(1, e)
    w_out = p["w_out"].astype(BF16)

    def branch(x, mod, tm, img_w, has_vert):
        a = _norm_matmul(x, mod, gain, w_in, tm=tm, tn=3 * e // 2)
        xconv, q, k, kt, gates = _conv_qk(a, cw, cb, wq, wk, wg, tm=tm, img_w=img_w, has_vert=has_vert)
        gcol, bcol, brow = _gate_fin(gates, bias, heads=heads)
        return dict(a=a, xconv=xconv, q=q, k=k, kt=kt, gcol=gcol, bcol=bcol, brow=brow)

    lat = branch(xl, mod_l, min(1024, n), GRID_W, True)
    ctx = branch(xc, mod_c, nc, nc, False)
    outs = _scan(lat, ctx, heads=heads, emit_ctx=update_ctx)
    xl = _mlstm_out(outs[0], outs[1], lat["a"], lat["xconv"], xl, lnw, skip, w_out, mod_l,
                    tm=min(256, n), heads=heads)
    if update_ctx:
        xc = _mlstm_out(outs[2], outs[3], ctx["a"], ctx["xconv"], xc, lnw, skip, w_out, mod_c,
                        tm=nc, heads=heads)
    return xl, xc


def _fourier_layer(xl, xc, mod_l, mod_c, gain, p, nf, *, update_ctx, final):
    groups = FOURIER_GROUPS
    n = xl.shape[1]
    nc = xc.shape[1]
    w_in = p["w_in"].astype(BF16)
    w_out = p["w_out"].astype(BF16)
    e = w_in.shape[1] // 2
    gd = e // groups
    cc, sc = _dft_mats(gd, gd ** -0.5)
    wc = jnp.concatenate([cc, sc], axis=1)

    def position_dft(uz, m):
        n1 = FFT_SLABS if m % (FFT_SLABS * 256) == 0 else 1
        if n1 == 1:
            a, bm = _chan_dft(uz, wc, tm=min(1024, m), groups=groups)
            cn, sn = _dft_mats(m, m ** -0.5)
            return _pos_dft(cn, sn, a, bm, tm=min(512, m), tn=min(512, e))
        n2 = m // n1
        k1 = lax.broadcasted_iota(jnp.int32, (n1, n2, 128), 0)
        p2 = lax.broadcasted_iota(jnp.int32, (n1, n2, 128), 1)
        ang = ((k1 * p2) % m).astype(F32) * (2.0 * np.pi / m)
        gr, gi = _fft_a(uz, jnp.concatenate([cc, -sc], axis=1), jnp.cos(ang), jnp.sin(ang),
                        n1=n1, groups=groups, tr=min(512, n2))
        w2c, w2s = _dft_mats(n2, m ** -0.5)
        return _fft_b(w2c, w2s, gr, gi, tn=256)

    def branch(x, mod, tm, fin):
        m = x.shape[1]
        uz = _norm_matmul(x, mod, gain, w_in, tm=tm, tn=e)
        y = position_dft(uz, m)
        return _fnet_out(y, uz, x, w_out, mod, nf, tm=min(512, m), final=fin)

    xl = branch(xl, mod_l, min(1024, n), final)
    if update_ctx:
        xc = branch(xc, mod_c, nc, False)
    return xl, xc


def kernel(x, c, ctx, c_ctx, norm_g, w_ada, b_ada, m_w_in, m_conv_w, m_conv_b, m_w_q, m_w_k, m_w_if, m_b_i, m_b_f,
           m_ln_w, m_skip, m_w_out, f_w_in, f_w_out, norm_f):
    b, n, d = x.shape
    depth = norm_g.shape[0]
    assert depth % N_MIXERS == 0, "the final norm is fused into the last Fourier layer"
    last_mlstm = ((depth - 1) // N_MIXERS) * N_MIXERS
    rows = -(-(b + 1) // 8) * 8
    cc = jnp.concatenate([c, c_ctx[None, :], jnp.zeros((rows - b - 1, d), F32)], axis=0)
    mods = _ada(cc, w_ada, b_ada)
    nf = norm_f.reshape(1, d)
    xl, xc = x, ctx
    for i in range(depth):
        j = i // N_MIXERS
        update_ctx = i < last_mlstm
        mod_l = mods[i, :b][:, None, :]
        mod_c = mods[i, b:b + 1][:, None, :]
        gain = norm_g[i].reshape(1, d)
        if i % N_MIXERS == 0:
            p = dict(w_in=m_w_in[j], conv_w=m_conv_w[j], conv_b=m_conv_b[j], w_q=m_w_q[j], w_k=m_w_k[j],
                     w_if=m_w_if[j], b_i=m_b_i[j], b_f=m_b_f[j], ln_w=m_ln_w[j], skip=m_skip[j], w_out=m_w_out[j])
            xl, xc = _mlstm_layer(xl, xc, mod_l, mod_c, gain, p, update_ctx=update_ctx)
        else:
            p = dict(w_in=f_w_in[j], w_out=f_w_out[j])
            xl, xc = _fourier_layer(xl, xc, mod_l, mod_c, gain, p, nf, update_ctx=update_ctx,
                                    final=(i == depth - 1))
    return xl
```
